```python
import math
import jax
import jax.numpy as jnp
from jax import lax
import numpy as np

D_MODEL = 4096
BATCH = 4
SEQ = 2048
DEPTH = 2
DEC_BATCH = 2
DEC_SEQ = 4096
PAST_LEN = 128

HEAD_DIM = 128
DIL_GROUPS = ((128, 1), (512, 4), (2048, 16))
A_HEADS_PER_GROUP = 4
A_HEADS = A_HEADS_PER_GROUP * len(DIL_GROUPS)
A_OUT = A_HEADS_PER_GROUP * HEAD_DIM
B_HEADS = 6
B_OUT = B_HEADS * 2 * HEAD_DIM
C_HEADS = 12
C_OUT = C_HEADS * HEAD_DIM
MIX_WIDTH = 1536
N_BRANCHES = 3
W_IN_COLS = 9 * MIX_WIDTH + N_BRANCHES * D_MODEL
GRID_W = 64
NA_ROWS_MAX = 8
NA_COLS = 16
NA_COL_BLOCK = 16
NA_COL_SPAN = 32
Q_BLOCK = 128
N_EXPERTS = 16
EC_CAPACITY_FACTOR = 2
D_FF_EXPERT = D_MODEL // 2
ROPE_THETA = 10000.0
RMS_EPS = 1e-6
NEG_INF = -1e30
ATTN_SCALE = HEAD_DIM ** -0.5

kernel_name = 'hybrid_dilated_diff_natten_ec_encoder'


def rms_norm(x, g):
    xf = x.astype(jnp.float32)
    y = xf * lax.rsqrt(jnp.mean(xf * xf, axis=-1, keepdims=True) + RMS_EPS)
    return (y * g.astype(jnp.float32)).astype(x.dtype)


def rope_tables(seq_len):
    inv = 1.0 / (ROPE_THETA ** (jnp.arange(0, HEAD_DIM, 2, dtype=jnp.float32) / HEAD_DIM))
    ang = jnp.arange(seq_len, dtype=jnp.float32)[:, None] * inv[None, :]
    ang = jnp.concatenate([ang, ang], axis=-1)
    return jnp.cos(ang), jnp.sin(ang)


def apply_rope(x, cos, sin):
    s = x.shape[1]
    shp = (1, s) + (1,) * (x.ndim - 3) + (HEAD_DIM,)
    xf = x.astype(jnp.float32)
    x1, x2 = jnp.split(xf, 2, axis=-1)
    rot = jnp.concatenate([-x2, x1], axis=-1)
    return (xf * cos.reshape(shp) + rot * sin.reshape(shp)).astype(x.dtype)


def banded_attention(q, k, v, radius):
    n, length, h, d = q.shape
    nb = -(-length // radius)
    lp = nb * radius
    qb = jnp.pad(q, ((0, 0), (0, lp - length), (0, 0), (0, 0))).reshape(n, nb, radius, h, d)
    kv_pad = ((0, 0), (radius, lp - length + radius), (0, 0), (0, 0))
    kb = jnp.pad(k, kv_pad).reshape(n, nb + 2, radius, h, d)
    vb = jnp.pad(v, kv_pad).reshape(n, nb + 2, radius, h, d)
    kw = jnp.concatenate([kb[:, :-2], kb[:, 1:-1], kb[:, 2:]], axis=2)
    vw = jnp.concatenate([vb[:, :-2], vb[:, 1:-1], vb[:, 2:]], axis=2)
    qpos = np.arange(lp).reshape(nb, radius)
    kpos = np.arange(nb)[:, None] * radius - radius + np.arange(3 * radius)[None, :]
    kp = kpos[:, None, :]
    mask = (np.abs(kp - qpos[:, :, None]) <= radius) & (kp >= 0) & (kp < length)
    sc = jnp.einsum('nbqhd,nbkhd->nbhqk', qb.astype(jnp.float32), kw.astype(jnp.float32)) * ATTN_SCALE
    sc = jnp.where(mask[None, :, None], sc, NEG_INF)
    lse = jax.nn.logsumexp(sc, axis=-1)
    p = jnp.exp(sc - lse[..., None])
    o = jnp.einsum('nbhqk,nbkhd->nbqhd', p, vw.astype(jnp.float32))
    o = o.reshape(n, lp, h, d)[:, :length]
    lse = lse.transpose(0, 1, 3, 2).reshape(n, lp, h)[:, :length]
    return o, lse


def dilated_attention(q, k, v):
    b, s, _, d = q.shape
    hg = A_HEADS_PER_GROUP
    outs, lses = [], []
    for g, (window, dil) in enumerate(DIL_GROUPS):
        def by_residue(t):
            t = t[:, :, g * hg:(g + 1) * hg].reshape(b, s // dil, dil, hg, d)
            return t.transpose(0, 2, 1, 3, 4).reshape(b * dil, s // dil, hg, d)
        o, l = banded_attention(by_residue(q), by_residue(k), by_residue(v), (window // 2) // dil)
        outs.append(o.reshape(b, dil, s // dil, hg, d).transpose(0, 2, 1, 3, 4).reshape(b, s, hg, d))
        lses.append(l.reshape(b, dil, s // dil, hg).transpose(0, 2, 1, 3).reshape(b, s, hg))
    w = jax.nn.softmax(jnp.stack(lses, axis=0), axis=0)
    out = jnp.sum(w[..., None] * jnp.stack(outs, axis=0), axis=0)
    return out.reshape(b, s, A_OUT).astype(q.dtype)


def diff_attention(q, k, v, lam, lam_init, subln_g):
    b, s, h, _, d = q.shape
    nblk = s // Q_BLOCK
    qb = q.reshape(b, nblk, Q_BLOCK, h, 2, d).transpose(1, 0, 2, 3, 4, 5)
    kf = k.astype(jnp.float32)
    vf = v.astype(jnp.float32)

    def block(qblk):
        sc = jnp.einsum('bqhcd,bkhcd->bchqk', qblk.astype(jnp.float32), kf) * ATTN_SCALE
        p = jax.nn.softmax(sc, axis=-1)
        a = p[:, 0] - lam * p[:, 1]
        return jnp.einsum('bhqk,bkhe->bqhe', a, vf)

    o = lax.map(block, qb)
    o = o.transpose(1, 0, 2, 3, 4).reshape(b, s, h, 2 * d)
    o = rms_norm(o, subln_g) * (1.0 - lam_init)
    return o.reshape(b, s, B_OUT).astype(q.dtype)


def neighborhood_attention(q, k, v, rpb):
    b, s, h, d = q.shape
    rows = s // GRID_W
    kr = min(NA_ROWS_MAX, rows)
    ncb = GRID_W // NA_COL_BLOCK
    cols = np.arange(GRID_W)
    qcols = cols.reshape(ncb, NA_COL_BLOCK)
    col_start = np.clip(cols - NA_COLS // 2, 0, GRID_W - NA_COLS).reshape(ncb, NA_COL_BLOCK)
    span_start = np.clip(np.arange(ncb) * NA_COL_BLOCK - NA_COLS // 2, 0, GRID_W - NA_COL_SPAN)
    span_cols = span_start[:, None] + np.arange(NA_COL_SPAN)[None, :]
    kcols = span_cols[:, None, :]
    col_mask = (kcols >= col_start[:, :, None]) & (kcols < col_start[:, :, None] + NA_COLS)
    dcol = np.clip(kcols - qcols[:, :, None], -(NA_COLS - 1), NA_COLS - 1) + NA_COLS - 1
    row_idx = np.arange(rows)
    row_start = np.clip(row_idx - kr // 2, 0, rows - kr)
    drow = row_start[:, None] + np.arange(kr)[None, :] - row_idx[:, None] + NA_ROWS_MAX - 1
    kg = k.reshape(b, rows, GRID_W, h, d).astype(jnp.float32)
    vg = v.reshape(b, rows, GRID_W, h, d).astype(jnp.float32)
    q_rows = q.reshape(b, rows, GRID_W, h, d).astype(jnp.float32).transpose(1, 0, 2, 3, 4)
    mask = jnp.asarray(col_mask)[None, :, None, :, None, :]
    rpb_f = rpb.astype(jnp.float32)

    def one_row(args):
        q_r, rs, dr = args
        k_r = lax.dynamic_slice_in_dim(kg, rs, kr, axis=1)[:, :, span_cols]
        v_r = lax.dynamic_slice_in_dim(vg, rs, kr, axis=1)[:, :, span_cols]
        qb = q_r.reshape(b, ncb, NA_COL_BLOCK, h, d)
        sc = jnp.einsum('bnqhd,brnkhd->bnhqrk', qb, k_r) * ATTN_SCALE
        bias = rpb_f[:, dr][:, :, dcol].transpose(2, 0, 3, 1, 4)
        sc = jnp.where(mask, sc + bias[None], NEG_INF)
        p = jax.nn.softmax(sc.reshape(b, ncb, h, NA_COL_BLOCK, kr * NA_COL_SPAN), axis=-1).reshape(sc.shape)
        o = jnp.einsum('bnhqrk,brnkhd->bnqhd', p, v_r)
        return o.reshape(b, GRID_W, h, d)

    out = lax.map(one_row, (q_rows, jnp.asarray(row_start, jnp.int32), jnp.asarray(drow, jnp.int32)))
    return out.transpose(1, 0, 2, 3, 4).reshape(b, s, C_OUT).astype(q.dtype)


def mixer_sublayer(x, l, cos, sin, g_mix, w_in, b_gate, qk_norm, lambda_qk, subln, rpb,
                   w_br_a, w_br_b, w_br_c, w_out):
    b, s, _ = x.shape
    h = rms_norm(x, g_mix)
    proj = jnp.einsum('bsd,dn->bsn', h, w_in)
    splits = [MIX_WIDTH * i for i in range(1, 10)]
    qa, ka, va, qb, kb, vb, qc, kc, vc, gate_logits = jnp.split(proj, splits, axis=-1)

    qa = apply_rope(rms_norm(qa.reshape(b, s, A_HEADS, HEAD_DIM), qk_norm[0]), cos, sin)
    ka = apply_rope(rms_norm(ka.reshape(b, s, A_HEADS, HEAD_DIM), qk_norm[1]), cos, sin)
    out_a = dilated_attention(qa, ka, va.reshape(b, s, A_HEADS, HEAD_DIM))

    qb = apply_rope(rms_norm(qb.reshape(b, s, B_HEADS, 2, HEAD_DIM), qk_norm[2]), cos, sin)
    kb = apply_rope(rms_norm(kb.reshape(b, s, B_HEADS, 2, HEAD_DIM), qk_norm[3]), cos, sin)
    lam_init = 0.8 - 0.6 * math.exp(-0.3 * l)
    lq = lambda_qk.astype(jnp.float32)
    lam = jnp.exp(jnp.sum(lq[0] * lq[1])) - jnp.exp(jnp.sum(lq[2] * lq[3])) + lam_init
    out_b = diff_attention(qb, kb, vb.reshape(b, s, B_HEADS, 2 * HEAD_DIM), lam, lam_init, subln)

    qc = rms_norm(qc.reshape(b, s, C_HEADS, HEAD_DIM), qk_norm[4])
    kc = rms_norm(kc.reshape(b, s, C_HEADS, HEAD_DIM), qk_norm[5])
    out_c = neighborhood_attention(qc, kc, vc.reshape(b, s, C_HEADS, HEAD_DIM), rpb)

    gates = jax.nn.sigmoid(gate_logits.astype(jnp.float32) + b_gate.astype(jnp.float32)).astype(x.dtype)
    ga, gb, gc = jnp.split(gates, N_BRANCHES, axis=-1)
    merged = (ga * jnp.einsum('bsc,cd->bsd', out_a, w_br_a)
              + gb * jnp.einsum('bsc,cd->bsd', out_b, w_br_b)
              + gc * jnp.einsum('bsc,cd->bsd', out_c, w_br_c))
    return jnp.einsum('bsd,de->bse', merged, w_out)


def expert_choice_ffn(h, w_router, w_gate, w_up, w_down):
    b, s, dm = h.shape
    n = b * s
    cap = EC_CAPACITY_FACTOR * n // N_EXPERTS
    hf = h.reshape(n, dm)
    aff = jax.nn.softmax(jnp.einsum('nd,de->ne', hf, w_router).astype(jnp.float32), axis=-1)
    gate, idx = lax.top_k(aff.T, cap)
    xe = hf[idx]
    hid = jax.nn.silu(jnp.einsum('ecd,edf->ecf', xe, w_gate)) * jnp.einsum('ecd,edf->ecf', xe, w_up)
    ye = jnp.einsum('ecf,efd->ecd', hid, w_down) * gate[..., None].astype(h.dtype)
    out = jnp.zeros_like(hf).at[idx.reshape(-1)].add(ye.reshape(-1, dm))
    return out.reshape(b, s, dm)


def trunk(x, g_mix, w_in, b_gate, qk_norm, lambda_qk, subln, rpb, w_br_a, w_br_b, w_br_c, w_out,
          g_ffn, w_router, w_e_gate, w_e_up, w_e_down):
    cos, sin = rope_tables(x.shape[1])
    for l in range(DEPTH):
        x = x + mixer_sublayer(x, l, cos, sin, g_mix[l], w_in[l], b_gate[l], qk_norm[l], lambda_qk[l],
                               subln[l], rpb[l], w_br_a[l], w_br_b[l], w_br_c[l], w_out[l])
        x = x + expert_choice_ffn(rms_norm(x, g_ffn[l]), w_router[l], w_e_gate[l], w_e_up[l], w_e_down[l])
    return x


def setup_inputs(seed: int = 0) -> dict:
    key = jax.random.key(seed)
    ks = jax.random.split(key, 20)

    def nrm(k, shape, scale):
        return jax.random.normal(k, shape, jnp.float32) * scale

    return {
        'x_prompt': nrm(ks[0], (BATCH, SEQ, D_MODEL), 1.0),
        'x_sample': nrm(ks[1], (DEC_BATCH, DEC_SEQ, D_MODEL), 1.0),
        'g_mix': 1.0 + nrm(ks[2], (DEPTH, D_MODEL), 0.02),
        'w_in': nrm(ks[3], (DEPTH, D_MODEL, W_IN_COLS), D_MODEL ** -0.5),
        'b_gate': nrm(ks[4], (DEPTH, N_BRANCHES * D_MODEL), 0.1),
        'qk_norm': 1.0 + nrm(ks[5], (DEPTH, 6, HEAD_DIM), 0.02),
        'lambda_qk': nrm(ks[6], (DEPTH, 4, HEAD_DIM), 0.1),
        'subln': 1.0 + nrm(ks[7], (DEPTH, 2 * HEAD_DIM), 0.02),
        'rpb': nrm(ks[8], (DEPTH, C_HEADS, 2 * NA_ROWS_MAX - 1, 2 * NA_COLS - 1), 0.1),
        'w_br_a': nrm(ks[9], (DEPTH, A_OUT, D_MODEL), A_OUT ** -0.5),
        'w_br_b': nrm(ks[10], (DEPTH, B_OUT, D_MODEL), B_OUT ** -0.5),
        'w_br_c': nrm(ks[11], (DEPTH, C_OUT, D_MODEL), C_OUT ** -0.5),
        'w_out': nrm(ks[12], (DEPTH, D_MODEL, D_MODEL), D_MODEL ** -0.5),
        'g_ffn': 1.0 + nrm(ks[13], (DEPTH, D_MODEL), 0.02),
        'w_router': nrm(ks[14], (DEPTH, D_MODEL, N_EXPERTS), D_MODEL ** -0.5),
        'w_e_gate': nrm(ks[15], (DEPTH, N_EXPERTS, D_MODEL, D_FF_EXPERT), D_MODEL ** -0.5),
        'w_e_up': nrm(ks[16], (DEPTH, N_EXPERTS, D_MODEL, D_FF_EXPERT), D_MODEL ** -0.5),
        'w_e_down': nrm(ks[17], (DEPTH, N_EXPERTS, D_FF_EXPERT, D_MODEL), D_FF_EXPERT ** -0.5),
    }


def reference(x_prompt, x_sample, g_mix, w_in, b_gate, qk_norm, lambda_qk, subln, rpb, w_br_a, w_br_b,
              w_br_c, w_out, g_ffn, w_router, w_e_gate, w_e_up, w_e_down):
    y_prompt = trunk(x_prompt, g_mix, w_in, b_gate, qk_norm, lambda_qk, subln, rpb, w_br_a, w_br_b, w_br_c,
                     w_out, g_ffn, w_router, w_e_gate, w_e_up, w_e_down)
    y_sample = trunk(x_sample, g_mix, w_in, b_gate, qk_norm, lambda_qk, subln, rpb, w_br_a, w_br_b, w_br_c,
                     w_out, g_ffn, w_router, w_e_gate, w_e_up, w_e_down)
    return (y_prompt, y_sample)
```

```python
import functools
import math
from typing import NamedTuple

import numpy as np
import jax
import jax.numpy as jnp
from jax import lax
from jax.experimental import pallas as pl
from jax.experimental.pallas import tpu as pltpu

HEAD = 128
MIXW = 1536
QKV_COLS = 9 * MIXW
A_GROUPS = ((64, 1), (256, 4), (1024, 16))
A_HPG = 4
B_HEADS = 6
C_HEADS = 12
GRID_W = 64
NA_ROWS = 8
NA_COLS = 16
C_ROWS_PER_BLOCK = 4
C_TQ = C_ROWS_PER_BLOCK * GRID_W
EPS = 1e-6
NEG = -1e30
SCALE = HEAD ** -0.5
ROPE_THETA = 10000.0
CAP_FACTOR = 2
SEL_CHUNK = 256
VMEM_LIMIT = 56 * 1024 * 1024

BF = jnp.bfloat16
F32 = jnp.float32


class Cfg(NamedTuple):
    d_model: int
    groups: tuple
    n_experts: int
    d_ff: int
    depth: int
    tm: int = 1024
    tn: int = 512
    tq_a: int = 128
    tq_b: int = 256
    tr: int = 256
    tf: int = 256


def _params(*sem):
    return pltpu.CompilerParams(dimension_semantics=sem, vmem_limit_bytes=VMEM_LIMIT)


def _rms(x):
    return x * lax.rsqrt(jnp.mean(x * x, axis=-1, keepdims=True) + EPS)


def _dot_t(a, b):
    return lax.dot_general(a, b, (((1,), (1,)), ((), ())), preferred_element_type=F32)


def _rmsnorm_kernel(x_ref, g_ref, o_ref):
    o_ref[...] = (_rms(x_ref[...]) * g_ref[...]).astype(o_ref.dtype)


def rmsnorm_bf16(x, g, tm):
    m, d = x.shape
    return pl.pallas_call(
        _rmsnorm_kernel,
        grid=(m // tm,),
        in_specs=[pl.BlockSpec((tm, d), lambda i: (i, 0)), pl.BlockSpec((1, d), lambda i: (0, 0))],
        out_specs=pl.BlockSpec((tm, d), lambda i: (i, 0)),
        out_shape=jax.ShapeDtypeStruct((m, d), BF),
        compiler_params=_params("parallel"),
        name="rmsnorm",
    )(x, g.reshape(1, d))


def _in_proj_kernel(h_ref, w_ref, nw_ref, b_ref, cos_ref, sin_ref, o_ref, *, tn):
    seg = (pl.program_id(1) * tn) // MIXW
    acc = jnp.dot(h_ref[...], w_ref[...].astype(BF), preferred_element_type=F32)
    heads = tn // HEAD

    def normed(h):
        sl = slice(h * HEAD, (h + 1) * HEAD)
        return _rms(acc[:, sl]) * nw_ref[:, sl]

    @pl.when((seg == 0) | (seg == 1) | (seg == 3) | (seg == 4))
    def _():
        for h in range(heads):
            y = normed(h)
            y = y * cos_ref[...] + pltpu.roll(y, HEAD // 2, 1) * sin_ref[...]
            o_ref[:, h * HEAD:(h + 1) * HEAD] = y.astype(o_ref.dtype)

    @pl.when((seg == 6) | (seg == 7))
    def _():
        for h in range(heads):
            o_ref[:, h * HEAD:(h + 1) * HEAD] = normed(h).astype(o_ref.dtype)

    @pl.when((seg == 2) | (seg == 5) | (seg == 8))
    def _():
        o_ref[...] = acc.astype(o_ref.dtype)

    @pl.when(seg >= 9)
    def _():
        o_ref[...] = jax.nn.sigmoid(acc + b_ref[...]).astype(o_ref.dtype)


def in_proj(h, w_in, nw, bias, cos, sin_signed, cfg):
    m, d = h.shape
    ncol = w_in.shape[1]
    tm, tn = cfg.tm, cfg.tn
    return pl.pallas_call(
        functools.partial(_in_proj_kernel, tn=tn),
        grid=(m // tm, ncol // tn),
        in_specs=[
            pl.BlockSpec((tm, d), lambda i, j: (i, 0)),
            pl.BlockSpec((d, tn), lambda i, j: (0, j)),
            pl.BlockSpec((1, tn), lambda i, j: (0, j)),
            pl.BlockSpec((1, tn), lambda i, j: (0, j)),
            pl.BlockSpec((tm, HEAD), lambda i, j: (i, 0)),
            pl.BlockSpec((tm, HEAD), lambda i, j: (i, 0)),
        ],
        out_specs=pl.BlockSpec((tm, tn), lambda i, j: (i, j)),
        out_shape=jax.ShapeDtypeStruct((m, ncol), BF),
        compiler_params=_params("parallel", "arbitrary"),
        name="in_proj",
    )(h, w_in, nw, bias, cos, sin_signed)


def _attn_a_kernel(q_ref, k_ref, v_ref, other_rows_hbm, o_ref, *, tq, seq):
    del other_rows_hbm
    t0 = pl.program_id(1) * tq
    geo = []
    for radius, dil in A_GROUPS:
        width = min(tq + 2 * radius, seq)
        start = pl.multiple_of(jnp.clip(t0 - radius, 0, seq - width), 64)
        diff = (start - t0) + lax.broadcasted_iota(jnp.int32, (tq, width), 1) \
            - lax.broadcasted_iota(jnp.int32, (tq, width), 0)
        valid = (jnp.abs(diff) <= radius) & ((diff & (dil - 1)) == 0)
        geo.append((start, width, valid))
    for hh in range(A_HPG):
        scores = []
        for g, (start, width, valid) in enumerate(geo):
            col = (g * A_HPG + hh) * HEAD
            q = q_ref[:, col:col + HEAD]
            k = k_ref[pl.ds(start, width), col:col + HEAD]
            scores.append(jnp.where(valid, _dot_t(q, k), NEG))
        mx = functools.reduce(jnp.maximum, [jnp.max(s, axis=-1, keepdims=True) for s in scores])
        den = jnp.zeros((tq, 1), F32)
        acc = jnp.zeros((tq, HEAD), F32)
        for g, (start, width, valid) in enumerate(geo):
            col = (g * A_HPG + hh) * HEAD
            p = jnp.exp(scores[g] - mx)
            den = den + jnp.sum(p, axis=-1, keepdims=True)
            v = v_ref[pl.ds(start, width), col:col + HEAD]
            acc = acc + jnp.dot(p.astype(BF), v, preferred_element_type=F32)
        o_ref[:, hh * HEAD:(hh + 1) * HEAD] = (acc / den).astype(o_ref.dtype)


def attn_a(proj, out, goff, batch, seq, cfg):
    tq = cfg.tq_a
    m = proj.shape[0]
    rb0, sb0 = goff // tq, goff // seq
    nq = seq // tq
    whole = dict(pipeline_mode=pl.Buffered(1))
    return pl.pallas_call(
        functools.partial(_attn_a_kernel, tq=tq, seq=seq),
        grid=(batch, nq),
        in_specs=[
            pl.BlockSpec((tq, MIXW), lambda b, i: (rb0 + b * nq + i, 0)),
            pl.BlockSpec((seq, MIXW), lambda b, i: (sb0 + b, 1), **whole),
            pl.BlockSpec((seq, MIXW), lambda b, i: (sb0 + b, 2), **whole),
            pl.BlockSpec(memory_space=pl.ANY),
        ],
        out_specs=pl.BlockSpec((tq, A_HPG * HEAD), lambda b, i: (rb0 + b * nq + i, 0)),
        out_shape=jax.ShapeDtypeStruct((m, A_HPG * HEAD), BF),
        input_output_aliases={3: 0},
        compiler_params=_params("parallel", "arbitrary"),
        name="attn_a",
    )(proj, proj, proj, out)


def _attn_b_kernel(q_ref, k_ref, v_ref, lam_ref, g_ref, other_rows_hbm, o_ref, *, lam_init):
    del other_rows_hbm
    lq = lam_ref[...]
    lam = (jnp.exp(jnp.sum(lq[0:1] * lq[1:2], keepdims=True))
           - jnp.exp(jnp.sum(lq[2:3] * lq[3:4], keepdims=True)) + lam_init)

    def probs(c):
        s = _dot_t(q_ref[:, c * HEAD:(c + 1) * HEAD], k_ref[:, c * HEAD:(c + 1) * HEAD])
        p = jnp.exp(s - jnp.max(s, axis=-1, keepdims=True))
        return p / jnp.sum(p, axis=-1, keepdims=True)

    a = (probs(0) - lam * probs(1)).astype(BF)
    o = jnp.dot(a, v_ref[...], preferred_element_type=F32)
    o_ref[...] = ((_rms(o) * g_ref[...]) * (1.0 - lam_init)).astype(o_ref.dtype)


def attn_b(proj, out, lambda_qk, subln, lam_init, goff, batch, seq, cfg):
    tq = min(cfg.tq_b, seq)
    m = proj.shape[0]
    w = 2 * HEAD
    rb0, sb0 = goff // tq, goff // seq
    nq = seq // tq
    c0 = 3 * MIXW // w
    return pl.pallas_call(
        functools.partial(_attn_b_kernel, lam_init=lam_init),
        grid=(batch, B_HEADS, nq),
        in_specs=[
            pl.BlockSpec((tq, w), lambda b, h, i: (rb0 + b * nq + i, c0 + h)),
            pl.BlockSpec((seq, w), lambda b, h, i: (sb0 + b, c0 + B_HEADS + h)),
            pl.BlockSpec((seq, w), lambda b, h, i: (sb0 + b, c0 + 2 * B_HEADS + h)),
            pl.BlockSpec((4, HEAD), lambda b, h, i: (0, 0)),
            pl.BlockSpec((1, w), lambda b, h, i: (0, 0)),
            pl.BlockSpec(memory_space=pl.ANY),
        ],
        out_specs=pl.BlockSpec((tq, w), lambda b, h, i: (rb0 + b * nq + i, h)),
        out_shape=jax.ShapeDtypeStruct((m, MIXW), BF),
        input_output_aliases={5: 0},
        compiler_params=_params("parallel", "parallel", "arbitrary"),
        name="attn_b",
    )(proj, proj, proj, lambda_qk, subln.reshape(1, w), out)


def _natten_bias_index():
    ql = np.arange(C_TQ)
    kl = np.arange(3 * C_TQ)
    qa, qj = ql // GRID_W, ql % GRID_W
    kc, ka, kj = kl // C_TQ - 1, (kl % C_TQ) // GRID_W, kl % GRID_W
    drow = (C_ROWS_PER_BLOCK * kc + ka)[None, :] - qa[:, None]
    cs = np.clip(qj - NA_COLS // 2, 0, GRID_W - NA_COLS)
    col_ok = (kj[None, :] >= cs[:, None]) & (kj[None, :] < cs[:, None] + NA_COLS)
    dcol = np.clip(kj[None, :] - qj[:, None], -(NA_COLS - 1), NA_COLS - 1) + NA_COLS - 1
    row_ok = np.stack([
        np.broadcast_to((kc >= 0)[None, :], drow.shape),
        (drow >= -(NA_ROWS // 2)) & (drow < NA_ROWS - NA_ROWS // 2),
        np.broadcast_to((kc <= 0)[None, :], drow.shape),
    ])
    flat = np.clip(drow + NA_ROWS - 1, 0, 2 * NA_ROWS - 2) * (2 * NA_COLS - 1) + dcol
    return flat.astype(np.int32), row_ok & col_ok[None]


def natten_bias(rpb):
    flat, ok = _natten_bias_index()
    vals = rpb.reshape(C_HEADS, -1)[:, flat]
    return jnp.where(ok[:, None], vals[None], NEG).astype(F32)


def _attn_c_kernel(q_ref, kp_ref, ko_ref, kn_ref, vp_ref, vo_ref, vn_ref, bias_ref, other_rows_hbm, o_ref):
    del other_rows_hbm
    heads = o_ref.shape[1] // HEAD
    for h in range(heads):
        sl = slice(h * HEAD, (h + 1) * HEAD)
        k = jnp.concatenate([kp_ref[:, sl], ko_ref[:, sl], kn_ref[:, sl]], axis=0)
        v = jnp.concatenate([vp_ref[:, sl], vo_ref[:, sl], vn_ref[:, sl]], axis=0)
        s = _dot_t(q_ref[:, sl], k) + bias_ref[h]
        p = jnp.exp(s - jnp.max(s, axis=-1, keepdims=True))
        o = jnp.dot(p.astype(BF), v, preferred_element_type=F32) / jnp.sum(p, axis=-1, keepdims=True)
        o_ref[:, sl] = o.astype(o_ref.dtype)


def attn_c(proj, out, bias, goff, batch, seq, cfg):
    m = proj.shape[0]
    wb = 4 * HEAD
    nhg = MIXW // wb
    nb = seq // C_TQ
    assert nb >= 3
    rb0 = goff // C_TQ
    cq, ck, cv = 6 * nhg, 7 * nhg, 8 * nhg

    def rows(b, i):
        return rb0 + b * nb + i

    def case(i):
        return jnp.where(i == 0, 0, jnp.where(i == nb - 1, 2, 1))

    def spec(c0, shift):
        return pl.BlockSpec((C_TQ, wb), lambda b, i, g: (rows(b, jnp.clip(i + shift, 0, nb - 1)), c0 + g))

    return pl.pallas_call(
        _attn_c_kernel,
        grid=(batch, nb, nhg),
        in_specs=[
            spec(cq, 0), spec(ck, -1), spec(ck, 0), spec(ck, 1), spec(cv, -1), spec(cv, 0), spec(cv, 1),
            pl.BlockSpec((None, 4, C_TQ, 3 * C_TQ), lambda b, i, g: (case(i), g, 0, 0)),
            pl.BlockSpec(memory_space=pl.ANY),
        ],
        out_specs=pl.BlockSpec((C_TQ, wb), lambda b, i, g: (rows(b, i), g)),
        out_shape=jax.ShapeDtypeStruct((m, MIXW), BF),
        input_output_aliases={8: 0},
        compiler_params=_params("parallel", "parallel", "arbitrary"),
        name="attn_c",
    )(proj, proj, proj, proj, proj, proj, proj, bias, out)


def _merge_kernel(oa_ref, ob_ref, oc_ref, wa_ref, wb_ref, wc_ref, ga_ref, gb_ref, gc_ref, o_ref):
    def br(o, w, g):
        return g[...].astype(F32) * jnp.dot(o[...], w[...].astype(BF), preferred_element_type=F32)

    o_ref[...] = (br(oa_ref, wa_ref, ga_ref) + br(ob_ref, wb_ref, gb_ref) + br(oc_ref, wc_ref, gc_ref)
                  ).astype(o_ref.dtype)


def merge(oa, ob, oc, wa, wb, wc, proj, cfg):
    m = oa.shape[0]
    d = wa.shape[1]
    tm, tn = cfg.tm, cfg.tn
    g0 = QKV_COLS // tn
    gs = d // tn

    def rowblk(width):
        return pl.BlockSpec((tm, width), lambda i, j: (i, 0))

    def wblk(k):
        return pl.BlockSpec((k, tn), lambda i, j: (0, j))

    def gate(n):
        return pl.BlockSpec((tm, tn), lambda i, j: (i, g0 + n * gs + j))

    return pl.pallas_call(
        _merge_kernel,
        grid=(m // tm, d // tn),
        in_specs=[rowblk(oa.shape[1]), rowblk(MIXW), rowblk(MIXW), wblk(wa.shape[0]), wblk(MIXW), wblk(MIXW),
                  gate(0), gate(1), gate(2)],
        out_specs=pl.BlockSpec((tm, tn), lambda i, j: (i, j)),
        out_shape=jax.ShapeDtypeStruct((m, d), BF),
        compiler_params=_params("parallel", "arbitrary"),
        name="merge",
    )(oa, ob, oc, wa, wb, wc, proj, proj, proj)


def _out_proj_kernel(m_ref, w_ref, x_ref, o_ref):
    o_ref[...] = x_ref[...] + jnp.dot(m_ref[...], w_ref[...].astype(BF), preferred_element_type=F32)


def out_proj(merged, w_out, x, cfg):
    m, d = x.shape
    tm, tn = cfg.tm, cfg.tn
    return pl.pallas_call(
        _out_proj_kernel,
        grid=(m // tm, d // tn),
        in_specs=[
            pl.BlockSpec((tm, d), lambda i, j: (i, 0)),
            pl.BlockSpec((d, tn), lambda i, j: (0, j)),
            pl.BlockSpec((tm, tn), lambda i, j: (i, j)),
        ],
        out_specs=pl.BlockSpec((tm, tn), lambda i, j: (i, j)),
        out_shape=jax.ShapeDtypeStruct((m, d), F32),
        compiler_params=_params("parallel", "arbitrary"),
        name="out_proj",
    )(merged, w_out, x)


def _router_kernel(x_ref, g_ref, w_ref, o_ref, *, n_experts):
    h = _rms(x_ref[...]) * g_ref[...]
    logits = jnp.dot(h, w_ref[...], preferred_element_type=F32, precision=lax.Precision.HIGHEST)
    lane = lax.broadcasted_iota(jnp.int32, logits.shape, 1)
    logits = jnp.where(lane < n_experts, logits, NEG)
    p = jnp.exp(logits - jnp.max(logits, axis=-1, keepdims=True))
    o_ref[...] = p / jnp.sum(p, axis=-1, keepdims=True)


def router(x, g, w_router, cfg):
    m, d = x.shape
    tm = min(512, cfg.tm)
    e = w_router.shape[1]
    wpad = jnp.zeros((d, HEAD), F32).at[:, :e].set(w_router)
    return pl.pallas_call(
        functools.partial(_router_kernel, n_experts=e),
        grid=(m // tm,),
        in_specs=[pl.BlockSpec((tm, d), lambda i: (i, 0)), pl.BlockSpec((1, d), lambda i: (0, 0)),
                  pl.BlockSpec((d, HEAD), lambda i: (0, 0))],
        out_specs=pl.BlockSpec((tm, HEAD), lambda i: (i, 0)),
        out_shape=jax.ShapeDtypeStruct((m, HEAD), F32),
        compiler_params=_params("parallel"),
        name="router",
    )(x, g.reshape(1, d), wpad)


def _prefix_incl(x01, upper):
    within = jnp.dot(x01.astype(BF), upper, preferred_element_type=F32)
    totals = jnp.broadcast_to(within[:, SEL_CHUNK - 1:SEL_CHUNK], within.shape)
    chunk = lax.broadcasted_iota(jnp.int32, within.shape, 0)
    before = jnp.zeros_like(within)
    for j in range(within.shape[0] - 1):
        before = before + jnp.where(chunk > j, totals[j:j + 1, :], 0.0)
    return within + before


def _topc_kernel(a_ref, idx_ref, gate_ref, *, cap):
    a = a_ref[...]
    nch = a.shape[0]
    bits = pltpu.bitcast(a, jnp.int32)

    def count(mask):
        return jnp.sum(jnp.where(mask, 1.0, 0.0), keepdims=True)

    def bit_step(i, t):
        cand = t | jnp.left_shift(jnp.int32(1), 30 - i)
        return jnp.where(count(bits >= cand) >= cap, cand, t)

    thr = lax.fori_loop(0, 31, bit_step, jnp.zeros((1, 1), jnp.int32))
    r_i = lax.broadcasted_iota(jnp.int32, (SEL_CHUNK, SEL_CHUNK), 0)
    c_i = lax.broadcasted_iota(jnp.int32, (SEL_CHUNK, SEL_CHUNK), 1)
    upper = jnp.where(r_i <= c_i, 1.0, 0.0).astype(BF)

    gt = bits > thr
    eq = bits == thr
    room = cap - count(gt)
    eq_f = jnp.where(eq, 1.0, 0.0)
    eq_before = _prefix_incl(eq_f, upper) - eq_f
    sel = gt | (eq & (eq_before < room))
    rank = jnp.where(sel, _prefix_incl(jnp.where(sel, 1.0, 0.0), upper), 0.0)
    tok = (lax.broadcasted_iota(jnp.int32, a.shape, 0) * SEL_CHUNK
           + lax.broadcasted_iota(jnp.int32, a.shape, 1)).astype(F32)

    rows = 64
    def slot_tile(ct, carry):
        want = (lax.broadcasted_iota(jnp.int32, (rows, SEL_CHUNK), 0) + (ct * rows + 1)).astype(F32)
        acc_i = jnp.zeros((rows, SEL_CHUNK), F32)
        acc_g = jnp.zeros((rows, SEL_CHUNK), F32)
        for j in range(nch):
            hit = rank[j:j + 1, :] == want
            acc_i = acc_i + jnp.where(hit, tok[j:j + 1, :], 0.0)
            acc_g = acc_g + jnp.where(hit, a[j:j + 1, :], 0.0)
        off = pl.multiple_of(ct * rows, rows)
        idx_ref[pl.ds(off, rows), :] = jnp.sum(acc_i, axis=-1, keepdims=True).astype(jnp.int32)
        gate_ref[pl.ds(off, rows), :] = jnp.sum(acc_g, axis=-1, keepdims=True)
        return carry

    lax.fori_loop(0, cap // rows, slot_tile, 0)


def topc(aff_t, cap):
    ge, nch, _ = aff_t.shape
    return pl.pallas_call(
        functools.partial(_topc_kernel, cap=cap),
        grid=(ge,),
        in_specs=[pl.BlockSpec((None, nch, SEL_CHUNK), lambda i: (i, 0, 0))],
        out_specs=[pl.BlockSpec((None, cap, 1), lambda i: (i, 0, 0)), pl.BlockSpec((None, cap, 1), lambda i: (i, 0, 0))],
        out_shape=[jax.ShapeDtypeStruct((ge, cap, 1), jnp.int32), jax.ShapeDtypeStruct((ge, cap, 1), F32)],
        compiler_params=_params("parallel"),
        name="topc",
    )(aff_t)


def _row_copy(src_hbm, dst, sem, tok, r):
    return pltpu.make_async_copy(src_hbm.at[pl.ds(tok, 1), :], dst.at[pl.ds(r, 1), :], sem)


def _gather_norm_kernel(idx_ref, x_hbm, g_ref, o_ref, buf, sem, *, tr):
    base = pl.program_id(0) * tr

    def start(r, c):
        _row_copy(x_hbm, buf, sem, idx_ref[base + r], r).start()
        return c

    def wait(r, c):
        _row_copy(x_hbm, buf, sem, idx_ref[base + r], r).wait()
        return c

    lax.fori_loop(0, tr, start, 0)
    lax.fori_loop(0, tr, wait, 0)
    o_ref[...] = (_rms(buf[...]) * g_ref[...]).astype(o_ref.dtype)


def gather_norm(x, g, idx, cfg):
    m, d = x.shape
    r = idx.shape[0]
    tr = cfg.tr
    return pl.pallas_call(
        functools.partial(_gather_norm_kernel, tr=tr),
        grid_spec=pltpu.PrefetchScalarGridSpec(
            num_scalar_prefetch=1,
            grid=(r // tr,),
            in_specs=[pl.BlockSpec(memory_space=pl.ANY), pl.BlockSpec((1, d), lambda i, idx: (0, 0))],
            out_specs=pl.BlockSpec((tr, d), lambda i, idx: (i, 0)),
            scratch_shapes=[pltpu.VMEM((tr, d), F32), pltpu.SemaphoreType.DMA(())],
        ),
        out_shape=jax.ShapeDtypeStruct((r, d), BF),
        compiler_params=_params("arbitrary"),
        name="gather_norm",
    )(idx, x, g.reshape(1, d))


def _ffn_up_kernel(x_ref, wg_ref, wu_ref, o_ref):
    x = x_ref[...]
    gate = jnp.dot(x, wg_ref[...].astype(BF), preferred_element_type=F32)
    up = jnp.dot(x, wu_ref[...].astype(BF), preferred_element_type=F32)
    o_ref[...] = (jax.nn.silu(gate) * up).astype(o_ref.dtype)


def ffn_up(xe, w_gate, w_up, cfg):
    e, d, f = w_gate.shape
    rows = xe.shape[0] // e
    tf = cfg.tf
    return pl.pallas_call(
        _ffn_up_kernel,
        grid=(e, f // tf),
        in_specs=[
            pl.BlockSpec((rows, d), lambda i, j: (i, 0), pipeline_mode=pl.Buffered(1)),
            pl.BlockSpec((None, d, tf), lambda i, j: (i, 0, j)),
            pl.BlockSpec((None, d, tf), lambda i, j: (i, 0, j)),
        ],
        out_specs=pl.BlockSpec((rows, tf), lambda i, j: (i, j)),
        out_shape=jax.ShapeDtypeStruct((xe.shape[0], f), BF),
        compiler_params=_params("parallel", "arbitrary"),
        name="ffn_up",
    )(xe, w_gate, w_up)


def _ffn_down_kernel(h_ref, w_ref, g_ref, o_ref):
    o_ref[...] = jnp.dot(h_ref[...], w_ref[...].astype(BF), preferred_element_type=F32) * g_ref[...]


def ffn_down(hid, w_down, gate, cfg):
    e, f, d = w_down.shape
    rows = hid.shape[0] // e
    tn = cfg.tn
    return pl.pallas_call(
        _ffn_down_kernel,
        grid=(e, d // tn),
        in_specs=[
            pl.BlockSpec((rows, f), lambda i, j: (i, 0)),
            pl.BlockSpec((None, f, tn), lambda i, j: (i, 0, j)),
            pl.BlockSpec((rows, 1), lambda i, j: (i, 0)),
        ],
        out_specs=pl.BlockSpec((rows, tn), lambda i, j: (i, j)),
        out_shape=jax.ShapeDtypeStruct((hid.shape[0], d), F32),
        compiler_params=_params("parallel", "arbitrary"),
        name="ffn_down",
    )(hid, w_down, gate)


def _scatter_add_kernel(idx_ref, y_ref, x_hbm, o_hbm, buf, sem, *, tr):
    del x_hbm
    base = pl.program_id(0) * tr

    def fetch(r):
        return pltpu.make_async_copy(o_hbm.at[pl.ds(idx_ref[base + r], 1), :], buf.at[pl.ds(r, 1), :], sem)

    def store(r):
        return pltpu.make_async_copy(buf.at[pl.ds(r, 1), :], o_hbm.at[pl.ds(idx_ref[base + r], 1), :], sem)

    def each(fn):
        def body(r, c):
            fn(r)
            return c
        lax.fori_loop(0, tr, body, 0)

    each(lambda r: fetch(r).start())
    each(lambda r: fetch(r).wait())
    buf[...] = buf[...] + y_ref[...]
    each(lambda r: store(r).start())
    each(lambda r: store(r).wait())


def scatter_add(x, ye, idx, cfg):
    m, d = x.shape
    r = idx.shape[0]
    tr = cfg.tr
    return pl.pallas_call(
        functools.partial(_scatter_add_kernel, tr=tr),
        grid_spec=pltpu.PrefetchScalarGridSpec(
            num_scalar_prefetch=1,
            grid=(r // tr,),
            in_specs=[pl.BlockSpec((tr, d), lambda i, idx: (i, 0)), pl.BlockSpec(memory_space=pl.ANY)],
            out_specs=pl.BlockSpec(memory_space=pl.ANY),
            scratch_shapes=[pltpu.VMEM((tr, d), F32), pltpu.SemaphoreType.DMA(())],
        ),
        out_shape=jax.ShapeDtypeStruct((m, d), F32),
        input_output_aliases={2: 0},
        compiler_params=_params("arbitrary"),
        name="scatter_add",
    )(idx, ye, x)


def _rope_tables(cfg):
    inv = 1.0 / (ROPE_THETA ** (jnp.arange(0, HEAD, 2, dtype=F32) / HEAD))
    cos, sin = [], []
    sign = jnp.where(jnp.arange(HEAD) < HEAD // 2, -1.0, 1.0).astype(F32)
    for batch, seq in cfg.groups:
        ang = jnp.arange(seq, dtype=F32)[:, None] * inv[None, :]
        ang = jnp.concatenate([ang, ang], axis=-1)
        cos.append(jnp.tile(jnp.cos(ang), (batch, 1)))
        sin.append(jnp.tile(jnp.sin(ang) * sign, (batch, 1)))
    return jnp.concatenate(cos, axis=0), jnp.concatenate(sin, axis=0)


def _group_offsets(cfg):
    offs, o = [], 0
    for batch, seq in cfg.groups:
        offs.append(o)
        o += batch * seq
    return offs, o


def _mixer(x, l, cos, sin, p, cfg):
    d = cfg.d_model
    m = x.shape[0]
    offs, _ = _group_offsets(cfg)
    ones = jnp.ones((MIXW,), F32)
    heads = MIXW // HEAD
    qk = p["qk_norm"][l].astype(F32)
    nw = jnp.concatenate([
        jnp.tile(qk[0] * SCALE, heads), jnp.tile(qk[1], heads), ones,
        jnp.tile(qk[2] * SCALE, heads), jnp.tile(qk[3], heads), ones,
        jnp.tile(qk[4] * SCALE, heads), jnp.tile(qk[5], heads), ones,
        jnp.ones((3 * d,), F32)]).reshape(1, -1)
    bias = jnp.concatenate([jnp.zeros((QKV_COLS,), F32), p["b_gate"][l].astype(F32)]).reshape(1, -1)

    h = rmsnorm_bf16(x, p["g_mix"][l], min(512, cfg.tm))
    proj = in_proj(h, p["w_in"][l], nw, bias, cos, sin, cfg)

    lam_init = 0.8 - 0.6 * math.exp(-0.3 * l)
    cbias = natten_bias(p["rpb"][l].astype(F32))
    oa = jnp.zeros((m, A_HPG * HEAD), BF)
    ob = jnp.zeros((m, MIXW), BF)
    oc = jnp.zeros((m, MIXW), BF)
    for (batch, seq), goff in zip(cfg.groups, offs):
        oa = attn_a(proj, oa, goff, batch, seq, cfg)
        ob = attn_b(proj, ob, p["lambda_qk"][l].astype(F32), p["subln"][l].astype(F32), lam_init, goff, batch, seq, cfg)
        oc = attn_c(proj, oc, cbias, goff, batch, seq, cfg)
    merged = merge(oa, ob, oc, p["w_br_a"][l], p["w_br_b"][l], p["w_br_c"][l], proj, cfg)
    return out_proj(merged, p["w_out"][l], x, cfg)


def _moe(x, l, p, cfg):
    e = cfg.n_experts
    offs, m = _group_offsets(cfg)
    aff = router(x, p["g_ffn"][l], p["w_router"][l], cfg)[:, :e]
    idx, gate = [], []
    caps = []
    for (batch, seq), goff in zip(cfg.groups, offs):
        n = batch * seq
        cap = CAP_FACTOR * n // e
        caps.append(cap)
        a_t = aff[goff:goff + n].T.reshape(e, n // SEL_CHUNK, SEL_CHUNK)
        i_g, g_g = topc(a_t, cap)
        idx.append(i_g.reshape(e, cap) + goff)
        gate.append(g_g.reshape(e, cap))
    idx = jnp.concatenate(idx, axis=1).reshape(-1)
    gate = jnp.concatenate(gate, axis=1).reshape(-1, 1)
    xe = gather_norm(x, p["g_ffn"][l], idx, cfg)
    hid = ffn_up(xe, p["w_e_gate"][l], p["w_e_up"][l], cfg)
    ye = ffn_down(hid, p["w_e_down"][l], gate, cfg)
    return scatter_add(x, ye, idx, cfg)


def trunk(xs, p, cfg):
    d = cfg.d_model
    x = jnp.concatenate([a.reshape(-1, d) for a in xs], axis=0)
    cos, sin = _rope_tables(cfg)
    for l in range(cfg.depth):
        x = _mixer(x, l, cos, sin, p, cfg)
        x = _moe(x, l, p, cfg)
    offs, _ = _group_offsets(cfg)
    return tuple(x[o:o + b * s].reshape(b, s, d) for (b, s), o in zip(cfg.groups, offs))


def kernel(x_prompt, x_sample, g_mix, w_in, b_gate, qk_norm, lambda_qk, subln, rpb, w_br_a, w_br_b, w_br_c,
           w_out, g_ffn, w_router, w_e_gate, w_e_up, w_e_down):
    cfg = Cfg(d_model=x_prompt.shape[-1],
              groups=(x_prompt.shape[:2], x_sample.shape[:2]),
              n_experts=w_router.shape[-1], d_ff=w_e_gate.shape[-1], depth=g_mix.shape[0])
    p = dict(g_mix=g_mix, w_in=w_in, b_gate=b_gate, qk_norm=qk_norm, lambda_qk=lambda_qk, subln=subln, rpb=rpb,
             w_br_a=w_br_a, w_br_b=w_br_b, w_br_c=w_br_c, w_out=w_out, g_ffn=g_ffn, w_router=w_router,
             w_e_gate=w_e_gate, w_e_up=w_e_up, w_e_down=w_e_down)
    return trunk((x_prompt, x_sample), p, cfg)
```

```python
import functools
import math
from typing import NamedTuple

import numpy as np
import jax
import jax.numpy as jnp
from jax import lax
from jax.experimental import pallas as pl
from jax.experimental.pallas import tpu as pltpu

HEAD = 128
MIXW = 1536
QKV_COLS = 9 * MIXW
A_GROUPS = ((64, 1), (256, 4), (1024, 16))
A_HPG = 4
B_HEADS = 6
C_HEADS = 12
GRID_W = 64
NA_ROWS = 8
NA_COLS = 16
C_ROWS_PER_BLOCK = 4
C_TQ = C_ROWS_PER_BLOCK * GRID_W
EPS = 1e-6
NEG = -1e30
LOG2E = math.log2(math.e)
Q_SCALE = HEAD ** -0.5 * LOG2E
ROPE_THETA = 10000.0
CAP_FACTOR = 2
SEL_CHUNK = 256
VMEM_LIMIT = 56 * 1024 * 1024

BF = jnp.bfloat16
F32 = jnp.float32


class Cfg(NamedTuple):
    d_model: int
    groups: tuple
    n_experts: int
    d_ff: int
    depth: int
    tm: int = 1024
    tn: int = 512
    tq_a: int = 128
    tq_b: int = 512
    tr: int = 256
    tf: int = 256


def _params(*sem):
    return pltpu.CompilerParams(dimension_semantics=sem, vmem_limit_bytes=VMEM_LIMIT)


def _rms(x):
    return x * lax.rsqrt(jnp.mean(x * x, axis=-1, keepdims=True) + EPS)


def _dot_t(a, b):
    return lax.dot_general(a, b, (((1,), (1,)), ((), ())), preferred_element_type=F32)


def _rmsnorm_kernel(x_ref, g_ref, o_ref):
    o_ref[...] = (_rms(x_ref[...]) * g_ref[...]).astype(o_ref.dtype)


def rmsnorm_bf16(x, g, tm):
    m, d = x.shape
    return pl.pallas_call(
        _rmsnorm_kernel,
        grid=(m // tm,),
        in_specs=[pl.BlockSpec((tm, d), lambda i: (i, 0)), pl.BlockSpec((1, d), lambda i: (0, 0))],
        out_specs=pl.BlockSpec((tm, d), lambda i: (i, 0)),
        out_shape=jax.ShapeDtypeStruct((m, d), BF),
        compiler_params=_params("parallel"),
        name="rmsnorm",
    )(x, g.reshape(1, d))


def _in_proj_kernel(h_ref, w_ref, nw_ref, b_ref, cos_ref, sin_ref, o_ref, *, tn):
    seg = (pl.program_id(1) * tn) // MIXW
    acc = jnp.dot(h_ref[...], w_ref[...].astype(BF), preferred_element_type=F32)
    heads = tn // HEAD

    def normed(h):
        sl = slice(h * HEAD, (h + 1) * HEAD)
        return _rms(acc[:, sl]) * nw_ref[:, sl]

    @pl.when((seg == 0) | (seg == 1) | (seg == 3) | (seg == 4))
    def _():
        for h in range(heads):
            y = normed(h)
            y = y * cos_ref[...] + pltpu.roll(y, HEAD // 2, 1) * sin_ref[...]
            o_ref[:, h * HEAD:(h + 1) * HEAD] = y.astype(o_ref.dtype)

    @pl.when((seg == 6) | (seg == 7))
    def _():
        for h in range(heads):
            o_ref[:, h * HEAD:(h + 1) * HEAD] = normed(h).astype(o_ref.dtype)

    @pl.when((seg == 2) | (seg == 5) | (seg == 8))
    def _():
        o_ref[...] = acc.astype(o_ref.dtype)

    @pl.when(seg >= 9)
    def _():
        o_ref[...] = (0.5 * jnp.tanh(0.5 * (acc + b_ref[...])) + 0.5).astype(o_ref.dtype)


def in_proj(h, w_in, layer, nw, bias, cos, sin_signed, cfg):
    m, d = h.shape
    ncol = w_in.shape[2]
    tm, tn = cfg.tm, cfg.tn
    return pl.pallas_call(
        functools.partial(_in_proj_kernel, tn=tn),
        grid=(m // tm, ncol // tn),
        in_specs=[
            pl.BlockSpec((tm, d), lambda i, j: (i, 0)),
            pl.BlockSpec((None, d, tn), lambda i, j: (layer, 0, j)),
            pl.BlockSpec((1, tn), lambda i, j: (0, j)),
            pl.BlockSpec((1, tn), lambda i, j: (0, j)),
            pl.BlockSpec((tm, HEAD), lambda i, j: (i, 0)),
            pl.BlockSpec((tm, HEAD), lambda i, j: (i, 0)),
        ],
        out_specs=pl.BlockSpec((tm, tn), lambda i, j: (i, j)),
        out_shape=jax.ShapeDtypeStruct((m, ncol), BF),
        compiler_params=_params("parallel", "arbitrary"),
        name="in_proj",
    )(h, w_in, nw, bias, cos, sin_signed)


def _attn_a_kernel(q_ref, k_ref, v_ref, other_rows_hbm, o_ref, *, tq, seq):
    del other_rows_hbm
    t0 = pl.program_id(1) * tq
    geo = []
    for radius, dil in A_GROUPS:
        width = min(tq + 2 * radius, seq)
        start = pl.multiple_of(jnp.clip(t0 - radius, 0, seq - width), 64)
        diff = (start - t0) + lax.broadcasted_iota(jnp.int32, (tq, width), 1) \
            - lax.broadcasted_iota(jnp.int32, (tq, width), 0)
        valid = (jnp.abs(diff) <= radius) & ((diff & (dil - 1)) == 0)
        geo.append((start, width, valid))
    for hh in range(A_HPG):
        scores = []
        for g, (start, width, valid) in enumerate(geo):
            col = (g * A_HPG + hh) * HEAD
            q = q_ref[:, col:col + HEAD]
            k = k_ref[pl.ds(start, width), col:col + HEAD]
            scores.append(jnp.where(valid, _dot_t(q, k), NEG))
        mx = functools.reduce(jnp.maximum, [jnp.max(s, axis=-1, keepdims=True) for s in scores])
        den = jnp.zeros((tq, 1), F32)
        acc = jnp.zeros((tq, HEAD), F32)
        for g, (start, width, valid) in enumerate(geo):
            col = (g * A_HPG + hh) * HEAD
            p = jnp.exp2(scores[g] - mx)
            den = den + jnp.sum(p, axis=-1, keepdims=True)
            v = v_ref[pl.ds(start, width), col:col + HEAD]
            acc = acc + jnp.dot(p.astype(BF), v, preferred_element_type=F32)
        o_ref[:, hh * HEAD:(hh + 1) * HEAD] = (acc / den).astype(o_ref.dtype)


def attn_a(proj, out, goff, batch, seq, cfg):
    tq = cfg.tq_a
    m = proj.shape[0]
    rb0, sb0 = goff // tq, goff // seq
    nq = seq // tq
    whole = dict(pipeline_mode=pl.Buffered(1))
    return pl.pallas_call(
        functools.partial(_attn_a_kernel, tq=tq, seq=seq),
        grid=(batch, nq),
        in_specs=[
            pl.BlockSpec((tq, MIXW), lambda b, i: (rb0 + b * nq + i, 0)),
            pl.BlockSpec((seq, MIXW), lambda b, i: (sb0 + b, 1), **whole),
            pl.BlockSpec((seq, MIXW), lambda b, i: (sb0 + b, 2), **whole),
            pl.BlockSpec(memory_space=pl.ANY),
        ],
        out_specs=pl.BlockSpec((tq, A_HPG * HEAD), lambda b, i: (rb0 + b * nq + i, 0)),
        out_shape=jax.ShapeDtypeStruct((m, A_HPG * HEAD), BF),
        input_output_aliases={3: 0},
        compiler_params=_params("parallel", "arbitrary"),
        name="attn_a",
    )(proj, proj, proj, out)


def _attn_b_kernel(q_ref, k_ref, v_ref, lam_ref, g_ref, other_rows_hbm, o_ref, *, lam_init):
    del other_rows_hbm
    lq = lam_ref[...]
    lam = (jnp.exp(jnp.sum(lq[0:1] * lq[1:2], keepdims=True))
           - jnp.exp(jnp.sum(lq[2:3] * lq[3:4], keepdims=True)) + lam_init)

    def unnormalised(c):
        s = _dot_t(q_ref[:, c * HEAD:(c + 1) * HEAD], k_ref[:, c * HEAD:(c + 1) * HEAD])
        p = jnp.exp2(s - jnp.max(s, axis=-1, keepdims=True))
        return p, 1.0 / jnp.sum(p, axis=-1, keepdims=True)

    p0, r0 = unnormalised(0)
    p1, r1 = unnormalised(1)
    a = (p0 * r0 - p1 * (lam * r1)).astype(BF)
    o = jnp.dot(a, v_ref[...], preferred_element_type=F32)
    o_ref[...] = ((_rms(o) * g_ref[...]) * (1.0 - lam_init)).astype(o_ref.dtype)


def attn_b(proj, out, lambda_qk, subln, lam_init, goff, batch, seq, cfg):
    tq = min(cfg.tq_b, seq)
    m = proj.shape[0]
    w = 2 * HEAD
    rb0, sb0 = goff // tq, goff // seq
    nq = seq // tq
    c0 = 3 * MIXW // w
    return pl.pallas_call(
        functools.partial(_attn_b_kernel, lam_init=lam_init),
        grid=(batch, B_HEADS, nq),
        in_specs=[
            pl.BlockSpec((tq, w), lambda b, h, i: (rb0 + b * nq + i, c0 + h)),
            pl.BlockSpec((seq, w), lambda b, h, i: (sb0 + b, c0 + B_HEADS + h)),
            pl.BlockSpec((seq, w), lambda b, h, i: (sb0 + b, c0 + 2 * B_HEADS + h)),
            pl.BlockSpec((4, HEAD), lambda b, h, i: (0, 0)),
            pl.BlockSpec((1, w), lambda b, h, i: (0, 0)),
            pl.BlockSpec(memory_space=pl.ANY),
        ],
        out_specs=pl.BlockSpec((tq, w), lambda b, h, i: (rb0 + b * nq + i, h)),
        out_shape=jax.ShapeDtypeStruct((m, MIXW), BF),
        input_output_aliases={5: 0},
        compiler_params=_params("parallel", "parallel", "arbitrary"),
        name="attn_b",
    )(proj, proj, proj, lambda_qk, subln.reshape(1, w), out)


def _natten_bias_index():
    j = np.arange(GRID_W)
    toeplitz = np.clip(j[None, :] - j[:, None], -(NA_COLS - 1), NA_COLS - 1) + NA_COLS - 1
    cs = np.clip(j - NA_COLS // 2, 0, GRID_W - NA_COLS)
    col_ok = (j[None, :] >= cs[:, None]) & (j[None, :] < cs[:, None] + NA_COLS)
    a = np.arange(C_ROWS_PER_BLOCK)
    kb = np.arange(3 * C_ROWS_PER_BLOCK)
    drow = (kb[None, :] - C_ROWS_PER_BLOCK) - a[:, None]
    kc = kb // C_ROWS_PER_BLOCK - 1
    row_ok = np.stack([
        np.broadcast_to((kc >= 0)[None, :], drow.shape),
        (drow >= -(NA_ROWS // 2)) & (drow < NA_ROWS - NA_ROWS // 2),
        np.broadcast_to((kc <= 0)[None, :], drow.shape),
    ])
    ok = row_ok[:, :, None, :, None] & col_ok[None, None, :, None, :]
    rpb_row = np.clip(drow + NA_ROWS - 1, 0, 2 * NA_ROWS - 2)
    return toeplitz.astype(np.int32), rpb_row.astype(np.int32), ok.reshape(3, C_TQ, 3 * C_TQ)


def natten_bias(rpb):
    toeplitz, rpb_row, ok = _natten_bias_index()
    t = rpb[:, :, toeplitz]
    v = t[:, rpb_row]
    v = v.transpose(0, 1, 3, 2, 4).reshape(C_HEADS, C_TQ, 3 * C_TQ)
    return jnp.where(ok[:, None], v[None] * LOG2E, NEG).astype(F32)


def _attn_c_kernel(q_ref, kp_ref, ko_ref, kn_ref, vp_ref, vo_ref, vn_ref, bias_ref, other_rows_hbm, o_ref):
    del other_rows_hbm
    heads = o_ref.shape[1] // HEAD
    for h in range(heads):
        sl = slice(h * HEAD, (h + 1) * HEAD)
        k = jnp.concatenate([kp_ref[:, sl], ko_ref[:, sl], kn_ref[:, sl]], axis=0)
        v = jnp.concatenate([vp_ref[:, sl], vo_ref[:, sl], vn_ref[:, sl]], axis=0)
        s = _dot_t(q_ref[:, sl], k) + bias_ref[h]
        p = jnp.exp2(s - jnp.max(s, axis=-1, keepdims=True))
        o = jnp.dot(p.astype(BF), v, preferred_element_type=F32) / jnp.sum(p, axis=-1, keepdims=True)
        o_ref[:, sl] = o.astype(o_ref.dtype)


def attn_c(proj, out, bias, goff, batch, seq, cfg):
    m = proj.shape[0]
    wb = 4 * HEAD
    nhg = MIXW // wb
    nb = seq // C_TQ
    assert nb >= 3
    rb0 = goff // C_TQ
    cq, ck, cv = 6 * nhg, 7 * nhg, 8 * nhg

    def rows(b, i):
        return rb0 + b * nb + i

    def case(i):
        return jnp.where(i == 0, 0, jnp.where(i == nb - 1, 2, 1))

    def spec(c0, shift):
        return pl.BlockSpec((C_TQ, wb), lambda b, i, g: (rows(b, jnp.clip(i + shift, 0, nb - 1)), c0 + g))

    return pl.pallas_call(
        _attn_c_kernel,
        grid=(batch, nb, nhg),
        in_specs=[
            spec(cq, 0), spec(ck, -1), spec(ck, 0), spec(ck, 1), spec(cv, -1), spec(cv, 0), spec(cv, 1),
            pl.BlockSpec((None, 4, C_TQ, 3 * C_TQ), lambda b, i, g: (case(i), g, 0, 0)),
            pl.BlockSpec(memory_space=pl.ANY),
        ],
        out_specs=pl.BlockSpec((C_TQ, wb), lambda b, i, g: (rows(b, i), g)),
        out_shape=jax.ShapeDtypeStruct((m, MIXW), BF),
        input_output_aliases={8: 0},
        compiler_params=_params("parallel", "parallel", "arbitrary"),
        name="attn_c",
    )(proj, proj, proj, proj, proj, proj, proj, bias, out)


def _merge_kernel(oa_ref, ob_ref, oc_ref, wa_ref, wb_ref, wc_ref, ga_ref, gb_ref, gc_ref, o_ref):
    def br(o, w, g):
        return g[...].astype(F32) * jnp.dot(o[...], w[...].astype(BF), preferred_element_type=F32)

    o_ref[...] = (br(oa_ref, wa_ref, ga_ref) + br(ob_ref, wb_ref, gb_ref) + br(oc_ref, wc_ref, gc_ref)
                  ).astype(o_ref.dtype)


def merge(oa, ob, oc, wa, wb, wc, layer, proj, cfg):
    m = oa.shape[0]
    d = wa.shape[2]
    tm, tn = cfg.tm, cfg.tn
    g0 = QKV_COLS // tn
    gs = d // tn

    def rowblk(width):
        return pl.BlockSpec((tm, width), lambda i, j: (i, 0))

    def wblk(k):
        return pl.BlockSpec((None, k, tn), lambda i, j: (layer, 0, j))

    def gate(n):
        return pl.BlockSpec((tm, tn), lambda i, j: (i, g0 + n * gs + j))

    return pl.pallas_call(
        _merge_kernel,
        grid=(m // tm, d // tn),
        in_specs=[rowblk(oa.shape[1]), rowblk(MIXW), rowblk(MIXW), wblk(wa.shape[1]), wblk(MIXW), wblk(MIXW),
                  gate(0), gate(1), gate(2)],
        out_specs=pl.BlockSpec((tm, tn), lambda i, j: (i, j)),
        out_shape=jax.ShapeDtypeStruct((m, d), BF),
        compiler_params=_params("parallel", "arbitrary"),
        name="merge",
    )(oa, ob, oc, wa, wb, wc, proj, proj, proj)


def _out_proj_kernel(m_ref, w_ref, x_ref, o_ref):
    o_ref[...] = x_ref[...] + jnp.dot(m_ref[...], w_ref[...].astype(BF), preferred_element_type=F32)


def out_proj(merged, w_out, layer, x, cfg):
    m, d = x.shape
    tm, tn = cfg.tm, cfg.tn
    return pl.pallas_call(
        _out_proj_kernel,
        grid=(m // tm, d // tn),
        in_specs=[
            pl.BlockSpec((tm, d), lambda i, j: (i, 0)),
            pl.BlockSpec((None, d, tn), lambda i, j: (layer, 0, j)),
            pl.BlockSpec((tm, tn), lambda i, j: (i, j)),
        ],
        out_specs=pl.BlockSpec((tm, tn), lambda i, j: (i, j)),
        out_shape=jax.ShapeDtypeStruct((m, d), F32),
        compiler_params=_params("parallel", "arbitrary"),
        name="out_proj",
    )(merged, w_out, x)


def _router_kernel(x_ref, g_ref, w_ref, o_ref, *, n_experts):
    def split(v):
        hi = v.astype(BF)
        return hi, (v - hi.astype(F32)).astype(BF)

    h_hi, h_lo = split(_rms(x_ref[...]) * g_ref[...])
    w_hi, w_lo = split(w_ref[...])
    mm = functools.partial(jnp.dot, preferred_element_type=F32)
    logits = mm(h_hi, w_hi) + (mm(h_hi, w_lo) + mm(h_lo, w_hi))
    lane = lax.broadcasted_iota(jnp.int32, logits.shape, 1)
    logits = jnp.where(lane < n_experts, logits, NEG)
    p = jnp.exp(logits - jnp.max(logits, axis=-1, keepdims=True))
    o_ref[...] = p / jnp.sum(p, axis=-1, keepdims=True)


def router(x, g, w_router, cfg):
    m, d = x.shape
    tm = min(512, cfg.tm)
    e = w_router.shape[1]
    wpad = jnp.zeros((d, HEAD), F32).at[:, :e].set(w_router)
    return pl.pallas_call(
        functools.partial(_router_kernel, n_experts=e),
        grid=(m // tm,),
        in_specs=[pl.BlockSpec((tm, d), lambda i: (i, 0)), pl.BlockSpec((1, d), lambda i: (0, 0)),
                  pl.BlockSpec((d, HEAD), lambda i: (0, 0))],
        out_specs=pl.BlockSpec((tm, HEAD), lambda i: (i, 0)),
        out_shape=jax.ShapeDtypeStruct((m, HEAD), F32),
        compiler_params=_params("parallel"),
        name="router",
    )(x, g.reshape(1, d), wpad)


def _prefix_incl(x01, upper):
    within = jnp.dot(x01.astype(BF), upper, preferred_element_type=F32)
    totals = jnp.broadcast_to(within[:, SEL_CHUNK - 1:SEL_CHUNK], within.shape)
    chunk = lax.broadcasted_iota(jnp.int32, within.shape, 0)
    before = jnp.zeros_like(within)
    for j in range(within.shape[0] - 1):
        before = before + jnp.where(chunk > j, totals[j:j + 1, :], 0.0)
    return within + before


def _topc_kernel(a_ref, idx_ref, gate_ref, *, cap):
    a = a_ref[...]
    nch = a.shape[0]
    bits = pltpu.bitcast(a, jnp.int32)

    def count(mask):
        return jnp.sum(jnp.where(mask, 1.0, 0.0), keepdims=True)

    def bit_step(i, t):
        cand = t | jnp.left_shift(jnp.int32(1), 30 - i)
        return jnp.where(count(bits >= cand) >= cap, cand, t)

    thr = lax.fori_loop(0, 31, bit_step, jnp.zeros((1, 1), jnp.int32))
    r_i = lax.broadcasted_iota(jnp.int32, (SEL_CHUNK, SEL_CHUNK), 0)
    c_i = lax.broadcasted_iota(jnp.int32, (SEL_CHUNK, SEL_CHUNK), 1)
    upper = jnp.where(r_i <= c_i, 1.0, 0.0).astype(BF)

    gt = bits > thr
    eq = bits == thr
    room = cap - count(gt)
    eq_f = jnp.where(eq, 1.0, 0.0)
    eq_before = _prefix_incl(eq_f, upper) - eq_f
    sel = gt | (eq & (eq_before < room))
    rank = jnp.where(sel, _prefix_incl(jnp.where(sel, 1.0, 0.0), upper), 0.0)
    tok = (lax.broadcasted_iota(jnp.int32, a.shape, 0) * SEL_CHUNK
           + lax.broadcasted_iota(jnp.int32, a.shape, 1)).astype(F32)

    rows = 64
    def slot_tile(ct, carry):
        want = (lax.broadcasted_iota(jnp.int32, (rows, SEL_CHUNK), 0) + (ct * rows + 1)).astype(F32)
        acc_i = jnp.zeros((rows, SEL_CHUNK), F32)
        acc_g = jnp.zeros((rows, SEL_CHUNK), F32)
        for j in range(nch):
            hit = rank[j:j + 1, :] == want
            acc_i = acc_i + jnp.where(hit, tok[j:j + 1, :], 0.0)
            acc_g = acc_g + jnp.where(hit, a[j:j + 1, :], 0.0)
        off = pl.multiple_of(ct * rows, rows)
        idx_ref[pl.ds(off, rows), :] = jnp.sum(acc_i, axis=-1, keepdims=True).astype(jnp.int32)
        gate_ref[pl.ds(off, rows), :] = jnp.sum(acc_g, axis=-1, keepdims=True)
        return carry

    lax.fori_loop(0, cap // rows, slot_tile, 0)


def topc(aff_t, cap):
    ge, nch, _ = aff_t.shape
    return pl.pallas_call(
        functools.partial(_topc_kernel, cap=cap),
        grid=(ge,),
        in_specs=[pl.BlockSpec((None, nch, SEL_CHUNK), lambda i: (i, 0, 0))],
        out_specs=[pl.BlockSpec((None, cap, 1), lambda i: (i, 0, 0)), pl.BlockSpec((None, cap, 1), lambda i: (i, 0, 0))],
        out_shape=[jax.ShapeDtypeStruct((ge, cap, 1), jnp.int32), jax.ShapeDtypeStruct((ge, cap, 1), F32)],
        compiler_params=_params("parallel"),
        name="topc",
    )(aff_t)


def _row_copy(src_hbm, dst, sem, tok, r):
    return pltpu.make_async_copy(src_hbm.at[pl.ds(tok, 1), :], dst.at[pl.ds(r, 1), :], sem)


ROW_LOOP_UNROLL = 8


def _gather_norm_kernel(idx_ref, x_hbm, g_ref, o_ref, buf, sem, *, tr):
    i = pl.program_id(0)
    slot = i % 2

    def rows(tile, s, go):
        def body(r, c):
            cp = _row_copy(x_hbm, buf.at[s], sem.at[s], idx_ref[tile * tr + r], r)
            cp.start() if go else cp.wait()
            return c
        lax.fori_loop(0, tr, body, 0, unroll=ROW_LOOP_UNROLL)

    @pl.when(i == 0)
    def _():
        rows(0, 0, True)

    @pl.when(i + 1 < pl.num_programs(0))
    def _():
        rows(i + 1, 1 - slot, True)

    rows(i, slot, False)
    o_ref[...] = (_rms(buf[slot]) * g_ref[...]).astype(o_ref.dtype)


def gather_norm(x, g, idx, cfg):
    m, d = x.shape
    r = idx.shape[0]
    tr = cfg.tr
    return pl.pallas_call(
        functools.partial(_gather_norm_kernel, tr=tr),
        grid_spec=pltpu.PrefetchScalarGridSpec(
            num_scalar_prefetch=1,
            grid=(r // tr,),
            in_specs=[pl.BlockSpec(memory_space=pl.ANY), pl.BlockSpec((1, d), lambda i, idx: (0, 0))],
            out_specs=pl.BlockSpec((tr, d), lambda i, idx: (i, 0)),
            scratch_shapes=[pltpu.VMEM((2, tr, d), F32), pltpu.SemaphoreType.DMA((2,))],
        ),
        out_shape=jax.ShapeDtypeStruct((r, d), BF),
        compiler_params=_params("arbitrary"),
        name="gather_norm",
    )(idx, x, g.reshape(1, d))


def _ffn_up_kernel(x_ref, wg_ref, wu_ref, o_ref):
    x = x_ref[...]
    gate = jnp.dot(x, wg_ref[...].astype(BF), preferred_element_type=F32)
    up = jnp.dot(x, wu_ref[...].astype(BF), preferred_element_type=F32)
    o_ref[...] = (jax.nn.silu(gate) * up).astype(o_ref.dtype)


def ffn_up(xe, w_gate, w_up, layer, cfg):
    _, e, d, f = w_gate.shape
    rows = xe.shape[0] // e
    tf = cfg.tf
    return pl.pallas_call(
        _ffn_up_kernel,
        grid=(e, f // tf),
        in_specs=[
            pl.BlockSpec((rows, d), lambda i, j: (i, 0), pipeline_mode=pl.Buffered(1)),
            pl.BlockSpec((None, None, d, tf), lambda i, j: (layer, i, 0, j)),
            pl.BlockSpec((None, None, d, tf), lambda i, j: (layer, i, 0, j)),
        ],
        out_specs=pl.BlockSpec((rows, tf), lambda i, j: (i, j)),
        out_shape=jax.ShapeDtypeStruct((xe.shape[0], f), BF),
        compiler_params=_params("parallel", "arbitrary"),
        name="ffn_up",
    )(xe, w_gate, w_up)


def _ffn_down_kernel(h_ref, w_ref, g_ref, o_ref):
    o_ref[...] = jnp.dot(h_ref[...], w_ref[...].astype(BF), preferred_element_type=F32) * g_ref[...]


def ffn_down(hid, w_down, layer, gate, cfg):
    _, e, f, d = w_down.shape
    rows = hid.shape[0] // e
    tn = cfg.tn
    return pl.pallas_call(
        _ffn_down_kernel,
        grid=(e, d // tn),
        in_specs=[
            pl.BlockSpec((rows, f), lambda i, j: (i, 0)),
            pl.BlockSpec((None, None, f, tn), lambda i, j: (layer, i, 0, j)),
            pl.BlockSpec((rows, 1), lambda i, j: (i, 0)),
        ],
        out_specs=pl.BlockSpec((rows, tn), lambda i, j: (i, j)),
        out_shape=jax.ShapeDtypeStruct((hid.shape[0], d), F32),
        compiler_params=_params("parallel", "arbitrary"),
        name="ffn_down",
    )(hid, w_down, gate)


def _scatter_add_kernel(idx_ref, y_ref, x_hbm, o_hbm, buf, sem, *, tr):
    del x_hbm
    base = pl.program_id(0) * tr

    half = tr // 2
    store_sem = sem.at[2]

    def fetch(h, r):
        return pltpu.make_async_copy(o_hbm.at[pl.ds(idx_ref[base + r], 1), :], buf.at[pl.ds(r, 1), :], sem.at[h])

    def store(r):
        return pltpu.make_async_copy(buf.at[pl.ds(r, 1), :], o_hbm.at[pl.ds(idx_ref[base + r], 1), :], store_sem)

    def each(h, fn):
        def body(r, c):
            fn(r)
            return c
        lax.fori_loop(h * half, (h + 1) * half, body, 0, unroll=ROW_LOOP_UNROLL)

    for h in range(2):
        each(h, lambda r: fetch(h, r).start())
    for h in range(2):
        each(h, lambda r: fetch(h, r).wait())
        rows = pl.ds(h * half, half)
        buf[rows, :] = buf[rows, :] + y_ref[rows, :]
        each(h, lambda r: store(r).start())
    for h in range(2):
        each(h, lambda r: store(r).wait())


def scatter_add(x, ye, idx, cfg):
    m, d = x.shape
    r = idx.shape[0]
    tr = cfg.tr
    return pl.pallas_call(
        functools.partial(_scatter_add_kernel, tr=tr),
        grid_spec=pltpu.PrefetchScalarGridSpec(
            num_scalar_prefetch=1,
            grid=(r // tr,),
            in_specs=[pl.BlockSpec((tr, d), lambda i, idx: (i, 0)), pl.BlockSpec(memory_space=pl.ANY)],
            out_specs=pl.BlockSpec(memory_space=pl.ANY),
            scratch_shapes=[pltpu.VMEM((tr, d), F32), pltpu.SemaphoreType.DMA((3,))],
        ),
        out_shape=jax.ShapeDtypeStruct((m, d), F32),
        input_output_aliases={2: 0},
        compiler_params=_params("arbitrary"),
        name="scatter_add",
    )(idx, ye, x)


def _rope_tables(cfg):
    inv = 1.0 / (ROPE_THETA ** (jnp.arange(0, HEAD, 2, dtype=F32) / HEAD))
    cos, sin = [], []
    sign = jnp.where(jnp.arange(HEAD) < HEAD // 2, -1.0, 1.0).astype(F32)
    for batch, seq in cfg.groups:
        ang = jnp.arange(seq, dtype=F32)[:, None] * inv[None, :]
        ang = jnp.concatenate([ang, ang], axis=-1)
        cos.append(jnp.tile(jnp.cos(ang), (batch, 1)))
        sin.append(jnp.tile(jnp.sin(ang) * sign, (batch, 1)))
    return jnp.concatenate(cos, axis=0), jnp.concatenate(sin, axis=0)


def _group_offsets(cfg):
    offs, o = [], 0
    for batch, seq in cfg.groups:
        offs.append(o)
        o += batch * seq
    return offs, o


def _mixer(x, l, cos, sin, p, cfg):
    d = cfg.d_model
    m = x.shape[0]
    offs, _ = _group_offsets(cfg)
    ones = jnp.ones((MIXW,), F32)
    heads = MIXW // HEAD
    qk = p["qk_norm"][l].astype(F32)
    nw = jnp.concatenate([
        jnp.tile(qk[0] * Q_SCALE, heads), jnp.tile(qk[1], heads), ones,
        jnp.tile(qk[2] * Q_SCALE, heads), jnp.tile(qk[3], heads), ones,
        jnp.tile(qk[4] * Q_SCALE, heads), jnp.tile(qk[5], heads), ones,
        jnp.ones((3 * d,), F32)]).reshape(1, -1)
    bias = jnp.concatenate([jnp.zeros((QKV_COLS,), F32), p["b_gate"][l].astype(F32)]).reshape(1, -1)

    h = rmsnorm_bf16(x, p["g_mix"][l], min(512, cfg.tm))
    proj = in_proj(h, p["w_in"], l, nw, bias, cos, sin, cfg)

    lam_init = 0.8 - 0.6 * math.exp(-0.3 * l)
    cbias = natten_bias(p["rpb"][l].astype(F32))
    oa = jnp.zeros((m, A_HPG * HEAD), BF)
    ob = jnp.zeros((m, MIXW), BF)
    oc = jnp.zeros((m, MIXW), BF)
    for (batch, seq), goff in zip(cfg.groups, offs):
        oa = attn_a(proj, oa, goff, batch, seq, cfg)
        ob = attn_b(proj, ob, p["lambda_qk"][l].astype(F32), p["subln"][l].astype(F32), lam_init, goff, batch, seq, cfg)
        oc = attn_c(proj, oc, cbias, goff, batch, seq, cfg)
    merged = merge(oa, ob, oc, p["w_br_a"], p["w_br_b"], p["w_br_c"], l, proj, cfg)
    return out_proj(merged, p["w_out"], l, x, cfg)


def _moe(x, l, p, cfg):
    e = cfg.n_experts
    offs, m = _group_offsets(cfg)
    aff = router(x, p["g_ffn"][l], p["w_router"][l], cfg)[:, :e]
    idx, gate = [], []
    caps = []
    for (batch, seq), goff in zip(cfg.groups, offs):
        n = batch * seq
        cap = CAP_FACTOR * n // e
        caps.append(cap)
        a_t = aff[goff:goff + n].T.reshape(e, n // SEL_CHUNK, SEL_CHUNK)
        i_g, g_g = topc(a_t, cap)
        idx.append(i_g.reshape(e, cap) + goff)
        gate.append(g_g.reshape(e, cap))
    idx = jnp.concatenate(idx, axis=1).reshape(-1)
    gate = jnp.concatenate(gate, axis=1).reshape(-1, 1)
    xe = gather_norm(x, p["g_ffn"][l], idx, cfg)
    hid = ffn_up(xe, p["w_e_gate"], p["w_e_up"], l, cfg)
    ye = ffn_down(hid, p["w_e_down"], l, gate, cfg)
    return scatter_add(x, ye, idx, cfg)


def trunk(xs, p, cfg):
    d = cfg.d_model
    x = jnp.concatenate([a.reshape(-1, d) for a in xs], axis=0)
    cos, sin = _rope_tables(cfg)
    for l in range(cfg.depth):
        x = _mixer(x, l, cos, sin, p, cfg)
        x = _moe(x, l, p, cfg)
    offs, _ = _group_offsets(cfg)
    return tuple(x[o:o + b * s].reshape(b, s, d) for (b, s), o in zip(cfg.groups, offs))


def kernel(x_prompt, x_sample, g_mix, w_in, b_gate, qk_norm, lambda_qk, subln, rpb, w_br_a, w_br_b, w_br_c,
           w_out, g_ffn, w_router, w_e_gate, w_e_up, w_e_down):
    cfg = Cfg(d_model=x_prompt.shape[-1],
              groups=(x_prompt.shape[:2], x_sample.shape[:2]),
              n_experts=w_router.shape[-1], d_ff=w_e_gate.shape[-1], depth=g_mix.shape[0])
    p = dict(g_mix=g_mix, w_in=w_in, b_gate=b_gate, qk_norm=qk_norm, lambda_qk=lambda_qk, subln=subln, rpb=rpb,
             w_br_a=w_br_a, w_br_b=w_br_b, w_br_c=w_br_c, w_out=w_out, g_ffn=g_ffn, w_router=w_router,
             w_e_gate=w_e_gate, w_e_up=w_e_up, w_e_down=w_e_down)
    return trunk((x_prompt, x_sample), p, cfg)
```

```python
import functools
import math
from typing import NamedTuple

import numpy as np
import jax
import jax.numpy as jnp
from jax import lax
from jax.experimental import pallas as pl
from jax.experimental.pallas import tpu as pltpu

HEAD = 128
MIXW = 1536
QKV_COLS = 9 * MIXW
A_GROUPS = ((64, 1), (256, 4), (1024, 16))
A_HPG = 4
B_HEADS = 6
C_HEADS = 12
GRID_W = 64
NA_ROWS = 8
NA_COLS = 16
C_ROWS_PER_BLOCK = 4
C_TQ = C_ROWS_PER_BLOCK * GRID_W
EPS = 1e-6
NEG = -1e30
LOG2E = math.log2(math.e)
Q_SCALE = HEAD ** -0.5 * LOG2E
ROPE_THETA = 10000.0
CAP_FACTOR = 2
SEL_CHUNK = 256
VMEM_LIMIT = 56 * 1024 * 1024

BF = jnp.bfloat16
F32 = jnp.float32


class Cfg(NamedTuple):
    d_model: int
    groups: tuple
    n_experts: int
    d_ff: int
    depth: int
    tm: int = 1024
    tn: int = 512
    tm_tall: int = 2048
    tn_tall: int = 256
    tq_a: int = 128
    tq_b: int = 1024
    sub_b: int = 256
    tr: int = 256
    tf: int = 256


def _params(*sem):
    return pltpu.CompilerParams(dimension_semantics=sem, vmem_limit_bytes=VMEM_LIMIT)


def _rms(x):
    return x * lax.rsqrt(jnp.mean(x * x, axis=-1, keepdims=True) + EPS)


def _dot_t(a, b):
    return lax.dot_general(a, b, (((1,), (1,)), ((), ())), preferred_element_type=F32)


def _rmsnorm_kernel(x_ref, g_ref, o_ref):
    o_ref[...] = (_rms(x_ref[...]) * g_ref[...]).astype(o_ref.dtype)


def rmsnorm_bf16(x, g, tm):
    m, d = x.shape
    return pl.pallas_call(
        _rmsnorm_kernel,
        grid=(m // tm,),
        in_specs=[pl.BlockSpec((tm, d), lambda i: (i, 0)), pl.BlockSpec((1, d), lambda i: (0, 0))],
        out_specs=pl.BlockSpec((tm, d), lambda i: (i, 0)),
        out_shape=jax.ShapeDtypeStruct((m, d), BF),
        compiler_params=_params("parallel"),
        name="rmsnorm",
    )(x, g.reshape(1, d))


def _in_proj_kernel(h_ref, w_ref, nw_ref, b_ref, cos_ref, sin_ref, o_ref, *, tn):
    seg = (pl.program_id(1) * tn) // MIXW
    acc = jnp.dot(h_ref[...], w_ref[...].astype(BF), preferred_element_type=F32)
    heads = tn // HEAD

    def normed(h):
        sl = slice(h * HEAD, (h + 1) * HEAD)
        return _rms(acc[:, sl]) * nw_ref[:, sl]

    @pl.when((seg == 0) | (seg == 1) | (seg == 3) | (seg == 4))
    def _():
        for h in range(heads):
            y = normed(h)
            y = y * cos_ref[...] + pltpu.roll(y, HEAD // 2, 1) * sin_ref[...]
            o_ref[:, h * HEAD:(h + 1) * HEAD] = y.astype(o_ref.dtype)

    @pl.when((seg == 6) | (seg == 7))
    def _():
        for h in range(heads):
            o_ref[:, h * HEAD:(h + 1) * HEAD] = normed(h).astype(o_ref.dtype)

    @pl.when((seg == 2) | (seg == 5) | (seg == 8))
    def _():
        o_ref[...] = acc.astype(o_ref.dtype)

    @pl.when(seg >= 9)
    def _():
        o_ref[...] = (0.5 * jnp.tanh(0.5 * (acc + b_ref[...])) + 0.5).astype(o_ref.dtype)


def in_proj(h, w_in, layer, nw, bias, cos, sin_signed, cfg):
    m, d = h.shape
    ncol = w_in.shape[2]
    tm, tn = cfg.tm, cfg.tn
    return pl.pallas_call(
        functools.partial(_in_proj_kernel, tn=tn),
        grid=(m // tm, ncol // tn),
        in_specs=[
            pl.BlockSpec((tm, d), lambda i, j: (i, 0)),
            pl.BlockSpec((None, d, tn), lambda i, j: (layer, 0, j)),
            pl.BlockSpec((1, tn), lambda i, j: (0, j)),
            pl.BlockSpec((1, tn), lambda i, j: (0, j)),
            pl.BlockSpec((tm, HEAD), lambda i, j: (i, 0)),
            pl.BlockSpec((tm, HEAD), lambda i, j: (i, 0)),
        ],
        out_specs=pl.BlockSpec((tm, tn), lambda i, j: (i, j)),
        out_shape=jax.ShapeDtypeStruct((m, ncol), BF),
        compiler_params=_params("parallel", "arbitrary"),
        name="in_proj",
    )(h, w_in, nw, bias, cos, sin_signed)


def _attn_a_kernel(q_ref, k_ref, v_ref, other_rows_hbm, o_ref, *, tq, seq):
    del other_rows_hbm
    t0 = pl.program_id(1) * tq
    geo = []
    for radius, dil in A_GROUPS:
        width = min(tq + 2 * radius, seq)
        start = pl.multiple_of(jnp.clip(t0 - radius, 0, seq - width), 64)
        diff = (start - t0) + lax.broadcasted_iota(jnp.int32, (tq, width), 1) \
            - lax.broadcasted_iota(jnp.int32, (tq, width), 0)
        valid = (jnp.abs(diff) <= radius) & ((diff & (dil - 1)) == 0)
        geo.append((start, width, jnp.where(valid, 0.0, NEG)))
    for hh in range(A_HPG):
        scores = []
        for g, (start, width, mask) in enumerate(geo):
            col = (g * A_HPG + hh) * HEAD
            q = q_ref[:, col:col + HEAD]
            k = k_ref[pl.ds(start, width), col:col + HEAD]
            scores.append(_dot_t(q, k) + mask)
        mx = functools.reduce(jnp.maximum, [jnp.max(s, axis=-1, keepdims=True) for s in scores])
        den = jnp.zeros((tq, 1), F32)
        acc = jnp.zeros((tq, HEAD), F32)
        for g, (start, width, mask) in enumerate(geo):
            col = (g * A_HPG + hh) * HEAD
            p = jnp.exp2(scores[g] - mx)
            den = den + jnp.sum(p, axis=-1, keepdims=True)
            v = v_ref[pl.ds(start, width), col:col + HEAD]
            acc = acc + jnp.dot(p.astype(BF), v, preferred_element_type=F32)
        o_ref[:, hh * HEAD:(hh + 1) * HEAD] = (acc / den).astype(o_ref.dtype)


def attn_a(proj, out, goff, batch, seq, cfg):
    tq = cfg.tq_a
    m = proj.shape[0]
    rb0, sb0 = goff // tq, goff // seq
    nq = seq // tq
    whole = dict(pipeline_mode=pl.Buffered(1))
    return pl.pallas_call(
        functools.partial(_attn_a_kernel, tq=tq, seq=seq),
        grid=(batch, nq),
        in_specs=[
            pl.BlockSpec((tq, MIXW), lambda b, i: (rb0 + b * nq + i, 0)),
            pl.BlockSpec((seq, MIXW), lambda b, i: (sb0 + b, 1), **whole),
            pl.BlockSpec((seq, MIXW), lambda b, i: (sb0 + b, 2), **whole),
            pl.BlockSpec(memory_space=pl.ANY),
        ],
        out_specs=pl.BlockSpec((tq, A_HPG * HEAD), lambda b, i: (rb0 + b * nq + i, 0)),
        out_shape=jax.ShapeDtypeStruct((m, A_HPG * HEAD), BF),
        input_output_aliases={3: 0},
        compiler_params=_params("parallel", "arbitrary"),
        name="attn_a",
    )(proj, proj, proj, out)


def _attn_b_kernel(q_ref, k_ref, v_ref, lam_ref, g_ref, other_rows_hbm, o_ref, s_even, s_odd, *, lam_init, sub):
    del other_rows_hbm
    n = q_ref.shape[0] // sub
    lq = lam_ref[...]
    lam = (jnp.exp(jnp.sum(lq[0:1] * lq[1:2], keepdims=True))
           - jnp.exp(jnp.sum(lq[2:3] * lq[3:4], keepdims=True)) + lam_init)

    def rows_of(i):
        return pl.ds(pl.multiple_of(i * sub, sub), sub)

    def scores(i, dst):
        for c in range(2):
            dst[c] = _dot_t(q_ref[rows_of(i), c * HEAD:(c + 1) * HEAD], k_ref[:, c * HEAD:(c + 1) * HEAD])

    def finish(i, src):
        def unnormalised(c):
            s = src[c]
            p = jnp.exp2(s - jnp.max(s, axis=-1, keepdims=True))
            return p, 1.0 / jnp.sum(p, axis=-1, keepdims=True)

        p0, r0 = unnormalised(0)
        p1, r1 = unnormalised(1)
        a = (p0 * r0 - p1 * (lam * r1)).astype(BF)
        o = jnp.dot(a, v_ref[...], preferred_element_type=F32)
        o_ref[rows_of(i), :] = ((_rms(o) * g_ref[...]) * (1.0 - lam_init)).astype(o_ref.dtype)

    scores(0, s_even)

    def pair(j, carry):
        i = 2 * j
        scores(i + 1, s_odd)
        finish(i, s_even)
        scores(i + 2, s_even)
        finish(i + 1, s_odd)
        return carry

    lax.fori_loop(0, n // 2 - 1, pair, 0)
    scores(n - 1, s_odd)
    finish(n - 2, s_even)
    finish(n - 1, s_odd)


def attn_b(proj, out, lambda_qk, subln, lam_init, goff, batch, seq, cfg):
    tq = min(cfg.tq_b, seq)
    sub = cfg.sub_b
    assert tq % (2 * sub) == 0
    m = proj.shape[0]
    w = 2 * HEAD
    rb0, sb0 = goff // tq, goff // seq
    nq = seq // tq
    c0 = 3 * MIXW // w
    return pl.pallas_call(
        functools.partial(_attn_b_kernel, lam_init=lam_init, sub=sub),
        grid=(batch, B_HEADS, nq),
        in_specs=[
            pl.BlockSpec((tq, w), lambda b, h, i: (rb0 + b * nq + i, c0 + h)),
            pl.BlockSpec((seq, w), lambda b, h, i: (sb0 + b, c0 + B_HEADS + h)),
            pl.BlockSpec((seq, w), lambda b, h, i: (sb0 + b, c0 + 2 * B_HEADS + h)),
            pl.BlockSpec((4, HEAD), lambda b, h, i: (0, 0)),
            pl.BlockSpec((1, w), lambda b, h, i: (0, 0)),
            pl.BlockSpec(memory_space=pl.ANY),
        ],
        out_specs=pl.BlockSpec((tq, w), lambda b, h, i: (rb0 + b * nq + i, h)),
        out_shape=jax.ShapeDtypeStruct((m, MIXW), BF),
        scratch_shapes=[pltpu.VMEM((2, sub, seq), F32), pltpu.VMEM((2, sub, seq), F32)],
        input_output_aliases={5: 0},
        compiler_params=_params("parallel", "parallel", "arbitrary"),
        name="attn_b",
    )(proj, proj, proj, lambda_qk, subln.reshape(1, w), out)


def _natten_bias_index():
    j = np.arange(GRID_W)
    toeplitz = np.clip(j[None, :] - j[:, None], -(NA_COLS - 1), NA_COLS - 1) + NA_COLS - 1
    cs = np.clip(j - NA_COLS // 2, 0, GRID_W - NA_COLS)
    col_ok = (j[None, :] >= cs[:, None]) & (j[None, :] < cs[:, None] + NA_COLS)
    a = np.arange(C_ROWS_PER_BLOCK)
    kb = np.arange(3 * C_ROWS_PER_BLOCK)
    drow = (kb[None, :] - C_ROWS_PER_BLOCK) - a[:, None]
    kc = kb // C_ROWS_PER_BLOCK - 1
    row_ok = np.stack([
        np.broadcast_to((kc >= 0)[None, :], drow.shape),
        (drow >= -(NA_ROWS // 2)) & (drow < NA_ROWS - NA_ROWS // 2),
        np.broadcast_to((kc <= 0)[None, :], drow.shape),
    ])
    ok = row_ok[:, :, None, :, None] & col_ok[None, None, :, None, :]
    rpb_row = np.clip(drow + NA_ROWS - 1, 0, 2 * NA_ROWS - 2)
    return toeplitz.astype(np.int32), rpb_row.astype(np.int32), ok.reshape(3, C_TQ, 3 * C_TQ)


def natten_bias(rpb):
    toeplitz, rpb_row, ok = _natten_bias_index()
    t = rpb[:, :, toeplitz]
    v = t[:, rpb_row]
    v = v.transpose(0, 1, 3, 2, 4).reshape(C_HEADS, C_TQ, 3 * C_TQ)
    return jnp.where(ok[:, None], v[None] * LOG2E, NEG).astype(F32)


def _attn_c_kernel(q_ref, kp_ref, ko_ref, kn_ref, vp_ref, vo_ref, vn_ref, bias_ref, other_rows_hbm, o_ref):
    del other_rows_hbm
    heads = o_ref.shape[1] // HEAD
    for h in range(heads):
        sl = slice(h * HEAD, (h + 1) * HEAD)
        k = jnp.concatenate([kp_ref[:, sl], ko_ref[:, sl], kn_ref[:, sl]], axis=0)
        v = jnp.concatenate([vp_ref[:, sl], vo_ref[:, sl], vn_ref[:, sl]], axis=0)
        s = _dot_t(q_ref[:, sl], k) + bias_ref[h]
        p = jnp.exp2(s - jnp.max(s, axis=-1, keepdims=True))
        o = jnp.dot(p.astype(BF), v, preferred_element_type=F32) / jnp.sum(p, axis=-1, keepdims=True)
        o_ref[:, sl] = o.astype(o_ref.dtype)


def attn_c(proj, out, bias, goff, batch, seq, cfg):
    m = proj.shape[0]
    wb = 4 * HEAD
    nhg = MIXW // wb
    nb = seq // C_TQ
    assert nb >= 3
    rb0 = goff // C_TQ
    cq, ck, cv = 6 * nhg, 7 * nhg, 8 * nhg

    def rows(b, i):
        return rb0 + b * nb + i

    def case(i):
        return jnp.where(i == 0, 0, jnp.where(i == nb - 1, 2, 1))

    def spec(c0, shift):
        return pl.BlockSpec((C_TQ, wb), lambda b, i, g: (rows(b, jnp.clip(i + shift, 0, nb - 1)), c0 + g))

    return pl.pallas_call(
        _attn_c_kernel,
        grid=(batch, nb, nhg),
        in_specs=[
            spec(cq, 0), spec(ck, -1), spec(ck, 0), spec(ck, 1), spec(cv, -1), spec(cv, 0), spec(cv, 1),
            pl.BlockSpec((None, 4, C_TQ, 3 * C_TQ), lambda b, i, g: (case(i), g, 0, 0)),
            pl.BlockSpec(memory_space=pl.ANY),
        ],
        out_specs=pl.BlockSpec((C_TQ, wb), lambda b, i, g: (rows(b, i), g)),
        out_shape=jax.ShapeDtypeStruct((m, MIXW), BF),
        input_output_aliases={8: 0},
        compiler_params=_params("parallel", "parallel", "arbitrary"),
        name="attn_c",
    )(proj, proj, proj, proj, proj, proj, proj, bias, out)


def _merge_kernel(oa_ref, ob_ref, oc_ref, wa_ref, wb_ref, wc_ref, ga_ref, gb_ref, gc_ref, o_ref):
    def br(o, w, g):
        return g[...].astype(F32) * jnp.dot(o[...], w[...].astype(BF), preferred_element_type=F32)

    o_ref[...] = (br(oa_ref, wa_ref, ga_ref) + br(ob_ref, wb_ref, gb_ref) + br(oc_ref, wc_ref, gc_ref)
                  ).astype(o_ref.dtype)


def merge(oa, ob, oc, wa, wb, wc, layer, proj, cfg):
    m = oa.shape[0]
    d = wa.shape[2]
    tm, tn = cfg.tm_tall, cfg.tn_tall
    g0 = QKV_COLS // tn
    gs = d // tn

    def rowblk(width):
        return pl.BlockSpec((tm, width), lambda i, j: (i, 0), pipeline_mode=pl.Buffered(1))

    def wblk(k):
        return pl.BlockSpec((None, k, tn), lambda i, j: (layer, 0, j))

    def gate(n):
        return pl.BlockSpec((tm, tn), lambda i, j: (i, g0 + n * gs + j))

    return pl.pallas_call(
        _merge_kernel,
        grid=(m // tm, d // tn),
        in_specs=[rowblk(oa.shape[1]), rowblk(MIXW), rowblk(MIXW), wblk(wa.shape[1]), wblk(MIXW), wblk(MIXW),
                  gate(0), gate(1), gate(2)],
        out_specs=pl.BlockSpec((tm, tn), lambda i, j: (i, j)),
        out_shape=jax.ShapeDtypeStruct((m, d), BF),
        compiler_params=_params("parallel", "arbitrary"),
        name="merge",
    )(oa, ob, oc, wa, wb, wc, proj, proj, proj)


def _out_proj_kernel(m_ref, w_ref, x_ref, o_ref):
    o_ref[...] = x_ref[...] + jnp.dot(m_ref[...], w_ref[...].astype(BF), preferred_element_type=F32)


def out_proj(merged, w_out, layer, x, cfg):
    m, d = x.shape
    tm, tn = cfg.tm_tall, cfg.tn_tall
    return pl.pallas_call(
        _out_proj_kernel,
        grid=(m // tm, d // tn),
        in_specs=[
            pl.BlockSpec((tm, d), lambda i, j: (i, 0), pipeline_mode=pl.Buffered(1)),
            pl.BlockSpec((None, d, tn), lambda i, j: (layer, 0, j)),
            pl.BlockSpec((tm, tn), lambda i, j: (i, j)),
        ],
        out_specs=pl.BlockSpec((tm, tn), lambda i, j: (i, j)),
        out_shape=jax.ShapeDtypeStruct((m, d), F32),
        compiler_params=_params("parallel", "arbitrary"),
        name="out_proj",
    )(merged, w_out, x)


def _router_kernel(x_ref, g_ref, w_ref, o_ref, *, n_experts):
    def split(v):
        hi = v.astype(BF)
        return hi, (v - hi.astype(F32)).astype(BF)

    h_hi, h_lo = split(_rms(x_ref[...]) * g_ref[...])
    w_hi, w_lo = split(w_ref[...])
    mm = functools.partial(jnp.dot, preferred_element_type=F32)
    logits = mm(h_hi, w_hi) + (mm(h_hi, w_lo) + mm(h_lo, w_hi))
    lane = lax.broadcasted_iota(jnp.int32, logits.shape, 1)
    logits = jnp.where(lane < n_experts, logits, NEG)
    p = jnp.exp(logits - jnp.max(logits, axis=-1, keepdims=True))
    o_ref[...] = p / jnp.sum(p, axis=-1, keepdims=True)


def router(x, g, w_router, cfg):
    m, d = x.shape
    tm = min(512, cfg.tm)
    e = w_router.shape[1]
    wpad = jnp.zeros((d, HEAD), F32).at[:, :e].set(w_router)
    return pl.pallas_call(
        functools.partial(_router_kernel, n_experts=e),
        grid=(m // tm,),
        in_specs=[pl.BlockSpec((tm, d), lambda i: (i, 0)), pl.BlockSpec((1, d), lambda i: (0, 0)),
                  pl.BlockSpec((d, HEAD), lambda i: (0, 0))],
        out_specs=pl.BlockSpec((tm, HEAD), lambda i: (i, 0)),
        out_shape=jax.ShapeDtypeStruct((m, HEAD), F32),
        compiler_params=_params("parallel"),
        name="router",
    )(x, g.reshape(1, d), wpad)


def _prefix_incl(x01, upper):
    within = jnp.dot(x01.astype(BF), upper, preferred_element_type=F32)
    totals = jnp.broadcast_to(within[:, SEL_CHUNK - 1:SEL_CHUNK], within.shape)
    chunk = lax.broadcasted_iota(jnp.int32, within.shape, 0)
    before = jnp.zeros_like(within)
    for j in range(within.shape[0] - 1):
        before = before + jnp.where(chunk > j, totals[j:j + 1, :], 0.0)
    return within + before


def _topc_kernel(a_ref, idx_ref, gate_ref, *, cap):
    a = a_ref[...]
    nch = a.shape[0]
    bits = pltpu.bitcast(a, jnp.int32)

    def count(mask):
        return jnp.sum(jnp.where(mask, 1.0, 0.0), keepdims=True)

    def bit_step(i, t):
        cand = t | jnp.left_shift(jnp.int32(1), 30 - i)
        return jnp.where(count(bits >= cand) >= cap, cand, t)

    thr = lax.fori_loop(0, 31, bit_step, jnp.zeros((1, 1), jnp.int32))
    r_i = lax.broadcasted_iota(jnp.int32, (SEL_CHUNK, SEL_CHUNK), 0)
    c_i = lax.broadcasted_iota(jnp.int32, (SEL_CHUNK, SEL_CHUNK), 1)
    upper = jnp.where(r_i <= c_i, 1.0, 0.0).astype(BF)

    gt = bits > thr
    eq = bits == thr
    room = cap - count(gt)
    eq_f = jnp.where(eq, 1.0, 0.0)
    eq_before = _prefix_incl(eq_f, upper) - eq_f
    sel = gt | (eq & (eq_before < room))
    rank = jnp.where(sel, _prefix_incl(jnp.where(sel, 1.0, 0.0), upper), 0.0)
    tok = (lax.broadcasted_iota(jnp.int32, a.shape, 0) * SEL_CHUNK
           + lax.broadcasted_iota(jnp.int32, a.shape, 1)).astype(F32)

    rows = 64
    def slot_tile(ct, carry):
        want = (lax.broadcasted_iota(jnp.int32, (rows, SEL_CHUNK), 0) + (ct * rows + 1)).astype(F32)
        acc_i = jnp.zeros((rows, SEL_CHUNK), F32)
        acc_g = jnp.zeros((rows, SEL_CHUNK), F32)
        for j in range(nch):
            hit = rank[j:j + 1, :] == want
            acc_i = acc_i + jnp.where(hit, tok[j:j + 1, :], 0.0)
            acc_g = acc_g + jnp.where(hit, a[j:j + 1, :], 0.0)
        off = pl.multiple_of(ct * rows, rows)
        idx_ref[pl.ds(off, rows), :] = jnp.sum(acc_i, axis=-1, keepdims=True).astype(jnp.int32)
        gate_ref[pl.ds(off, rows), :] = jnp.sum(acc_g, axis=-1, keepdims=True)
        return carry

    lax.fori_loop(0, cap // rows, slot_tile, 0)


def topc(aff_t, cap):
    ge, nch, _ = aff_t.shape
    return pl.pallas_call(
        functools.partial(_topc_kernel, cap=cap),
        grid=(ge,),
        in_specs=[pl.BlockSpec((None, nch, SEL_CHUNK), lambda i: (i, 0, 0))],
        out_specs=[pl.BlockSpec((None, cap, 1), lambda i: (i, 0, 0)), pl.BlockSpec((None, cap, 1), lambda i: (i, 0, 0))],
        out_shape=[jax.ShapeDtypeStruct((ge, cap, 1), jnp.int32), jax.ShapeDtypeStruct((ge, cap, 1), F32)],
        compiler_params=_params("parallel"),
        name="topc",
    )(aff_t)


def _row_copy(src_hbm, dst, sem, tok, r):
    return pltpu.make_async_copy(src_hbm.at[pl.ds(tok, 1), :], dst.at[pl.ds(r, 1), :], sem)


ROW_LOOP_UNROLL = 8


def _gather_norm_kernel(idx_ref, x_hbm, g_ref, o_ref, buf, sem, *, tr):
    i = pl.program_id(0)
    slot = i % 2

    def rows(tile, s, go):
        def body(r, c):
            cp = _row_copy(x_hbm, buf.at[s], sem.at[s], idx_ref[tile * tr + r], r)
            cp.start() if go else cp.wait()
            return c
        lax.fori_loop(0, tr, body, 0, unroll=ROW_LOOP_UNROLL)

    @pl.when(i == 0)
    def _():
        rows(0, 0, True)

    @pl.when(i + 1 < pl.num_programs(0))
    def _():
        rows(i + 1, 1 - slot, True)

    rows(i, slot, False)
    o_ref[...] = (_rms(buf[slot]) * g_ref[...]).astype(o_ref.dtype)


def gather_norm(x, g, idx, cfg):
    m, d = x.shape
    r = idx.shape[0]
    tr = cfg.tr
    return pl.pallas_call(
        functools.partial(_gather_norm_kernel, tr=tr),
        grid_spec=pltpu.PrefetchScalarGridSpec(
            num_scalar_prefetch=1,
            grid=(r // tr,),
            in_specs=[pl.BlockSpec(memory_space=pl.ANY), pl.BlockSpec((1, d), lambda i, idx: (0, 0))],
            out_specs=pl.BlockSpec((tr, d), lambda i, idx: (i, 0)),
            scratch_shapes=[pltpu.VMEM((2, tr, d), F32), pltpu.SemaphoreType.DMA((2,))],
        ),
        out_shape=jax.ShapeDtypeStruct((r, d), BF),
        compiler_params=_params("arbitrary"),
        name="gather_norm",
    )(idx, x, g.reshape(1, d))


def _ffn_up_kernel(x_ref, wg_ref, wu_ref, o_ref):
    x = x_ref[...]
    gate = jnp.dot(x, wg_ref[...].astype(BF), preferred_element_type=F32)
    up = jnp.dot(x, wu_ref[...].astype(BF), preferred_element_type=F32)
    o_ref[...] = (jax.nn.silu(gate) * up).astype(o_ref.dtype)


def ffn_up(xe, w_gate, w_up, layer, cfg):
    _, e, d, f = w_gate.shape
    rows = xe.shape[0] // e
    tf = cfg.tf
    return pl.pallas_call(
        _ffn_up_kernel,
        grid=(e, f // tf),
        in_specs=[
            pl.BlockSpec((rows, d), lambda i, j: (i, 0), pipeline_mode=pl.Buffered(1)),
            pl.BlockSpec((None, None, d, tf), lambda i, j: (layer, i, 0, j)),
            pl.BlockSpec((None, None, d, tf), lambda i, j: (layer, i, 0, j)),
        ],
        out_specs=pl.BlockSpec((rows, tf), lambda i, j: (i, j)),
        out_shape=jax.ShapeDtypeStruct((xe.shape[0], f), BF),
        compiler_params=_params("parallel", "arbitrary"),
        name="ffn_up",
    )(xe, w_gate, w_up)


def _ffn_down_kernel(h_ref, w_ref, g_ref, o_ref):
    o_ref[...] = jnp.dot(h_ref[...], w_ref[...].astype(BF), preferred_element_type=F32) * g_ref[...]


def ffn_down(hid, w_down, layer, gate, cfg):
    _, e, f, d = w_down.shape
    rows = hid.shape[0] // e
    tn = cfg.tn
    return pl.pallas_call(
        _ffn_down_kernel,
        grid=(e, d // tn),
        in_specs=[
            pl.BlockSpec((rows, f), lambda i, j: (i, 0)),
            pl.BlockSpec((None, None, f, tn), lambda i, j: (layer, i, 0, j)),
            pl.BlockSpec((rows, 1), lambda i, j: (i, 0)),
        ],
        out_specs=pl.BlockSpec((rows, tn), lambda i, j: (i, j)),
        out_shape=jax.ShapeDtypeStruct((hid.shape[0], d), F32),
        compiler_params=_params("parallel", "arbitrary"),
        name="ffn_down",
    )(hid, w_down, gate)


def _scatter_add_kernel(idx_ref, y_ref, x_hbm, o_hbm, buf, sem, *, tr):
    del x_hbm
    base = pl.program_id(0) * tr

    half = tr // 2
    store_sem = sem.at[2]

    def fetch(h, r):
        return pltpu.make_async_copy(o_hbm.at[pl.ds(idx_ref[base + r], 1), :], buf.at[pl.ds(r, 1), :], sem.at[h])

    def store(r):
        return pltpu.make_async_copy(buf.at[pl.ds(r, 1), :], o_hbm.at[pl.ds(idx_ref[base + r], 1), :], store_sem)

    def each(h, fn):
        def body(r, c):
            fn(r)
            return c
        lax.fori_loop(h * half, (h + 1) * half, body, 0, unroll=ROW_LOOP_UNROLL)

    for h in range(2):
        each(h, lambda r: fetch(h, r).start())
    for h in range(2):
        each(h, lambda r: fetch(h, r).wait())
        rows = pl.ds(h * half, half)
        buf[rows, :] = buf[rows, :] + y_ref[rows, :]
        each(h, lambda r: store(r).start())
    for h in range(2):
        each(h, lambda r: store(r).wait())


def scatter_add(x, ye, idx, cfg):
    m, d = x.shape
    r = idx.shape[0]
    tr = cfg.tr
    return pl.pallas_call(
        functools.partial(_scatter_add_kernel, tr=tr),
        grid_spec=pltpu.PrefetchScalarGridSpec(
            num_scalar_prefetch=1,
            grid=(r // tr,),
            in_specs=[pl.BlockSpec((tr, d), lambda i, idx: (i, 0)), pl.BlockSpec(memory_space=pl.ANY)],
            out_specs=pl.BlockSpec(memory_space=pl.ANY),
            scratch_shapes=[pltpu.VMEM((tr, d), F32), pltpu.SemaphoreType.DMA((3,))],
        ),
        out_shape=jax.ShapeDtypeStruct((m, d), F32),
        input_output_aliases={2: 0},
        compiler_params=_params("arbitrary"),
        name="scatter_add",
    )(idx, ye, x)


def _rope_tables(cfg):
    inv = 1.0 / (ROPE_THETA ** (jnp.arange(0, HEAD, 2, dtype=F32) / HEAD))
    cos, sin = [], []
    sign = jnp.where(jnp.arange(HEAD) < HEAD // 2, -1.0, 1.0).astype(F32)
    for batch, seq in cfg.groups:
        ang = jnp.arange(seq, dtype=F32)[:, None] * inv[None, :]
        ang = jnp.concatenate([ang, ang], axis=-1)
        cos.append(jnp.tile(jnp.cos(ang), (batch, 1)))
        sin.append(jnp.tile(jnp.sin(ang) * sign, (batch, 1)))
    return jnp.concatenate(cos, axis=0), jnp.concatenate(sin, axis=0)


def _group_offsets(cfg):
    offs, o = [], 0
    for batch, seq in cfg.groups:
        offs.append(o)
        o += batch * seq
    return offs, o


def _mixer(x, l, cos, sin, p, cfg):
    d = cfg.d_model
    m = x.shape[0]
    offs, _ = _group_offsets(cfg)
    ones = jnp.ones((MIXW,), F32)
    heads = MIXW // HEAD
    qk = p["qk_norm"][l].astype(F32)
    nw = jnp.concatenate([
        jnp.tile(qk[0] * Q_SCALE, heads), jnp.tile(qk[1], heads), ones,
        jnp.tile(qk[2] * Q_SCALE, heads), jnp.tile(qk[3], heads), ones,
        jnp.tile(qk[4] * Q_SCALE, heads), jnp.tile(qk[5], heads), ones,
        jnp.ones((3 * d,), F32)]).reshape(1, -1)
    bias = jnp.concatenate([jnp.zeros((QKV_COLS,), F32), p["b_gate"][l].astype(F32)]).reshape(1, -1)

    h = rmsnorm_bf16(x, p["g_mix"][l], min(512, cfg.tm))
    proj = in_proj(h, p["w_in"], l, nw, bias, cos, sin, cfg)

    lam_init = 0.8 - 0.6 * math.exp(-0.3 * l)
    cbias = natten_bias(p["rpb"][l].astype(F32))
    oa = jnp.zeros((m, A_HPG * HEAD), BF)
    ob = jnp.zeros((m, MIXW), BF)
    oc = jnp.zeros((m, MIXW), BF)
    for (batch, seq), goff in zip(cfg.groups, offs):
        oa = attn_a(proj, oa, goff, batch, seq, cfg)
        ob = attn_b(proj, ob, p["lambda_qk"][l].astype(F32), p["subln"][l].astype(F32), lam_init, goff, batch, seq, cfg)
        oc = attn_c(proj, oc, cbias, goff, batch, seq, cfg)
    merged = merge(oa, ob, oc, p["w_br_a"], p["w_br_b"], p["w_br_c"], l, proj, cfg)
    return out_proj(merged, p["w_out"], l, x, cfg)


def _moe(x, l, p, cfg):
    e = cfg.n_experts
    offs, m = _group_offsets(cfg)
    aff = router(x, p["g_ffn"][l], p["w_router"][l], cfg)[:, :e]
    idx, gate = [], []
    caps = []
    for (batch, seq), goff in zip(cfg.groups, offs):
        n = batch * seq
        cap = CAP_FACTOR * n // e
        caps.append(cap)
        a_t = aff[goff:goff + n].T.reshape(e, n // SEL_CHUNK, SEL_CHUNK)
        i_g, g_g = topc(a_t, cap)
        idx.append(i_g.reshape(e, cap) + goff)
        gate.append(g_g.reshape(e, cap))
    idx = jnp.concatenate(idx, axis=1).reshape(-1)
    gate = jnp.concatenate(gate, axis=1).reshape(-1, 1)
    xe = gather_norm(x, p["g_ffn"][l], idx, cfg)
    hid = ffn_up(xe, p["w_e_gate"], p["w_e_up"], l, cfg)
    ye = ffn_down(hid, p["w_e_down"], l, gate, cfg)
    return scatter_add(x, ye, idx, cfg)


def trunk(xs, p, cfg):
    d = cfg.d_model
    x = jnp.concatenate([a.reshape(-1, d) for a in xs], axis=0)
    cos, sin = _rope_tables(cfg)
    for l in range(cfg.depth):
        x = _mixer(x, l, cos, sin, p, cfg)
        x = _moe(x, l, p, cfg)
    offs, _ = _group_offsets(cfg)
    return tuple(x[o:o + b * s].reshape(b, s, d) for (b, s), o in zip(cfg.groups, offs))


def kernel(x_prompt, x_sample, g_mix, w_in, b_gate, qk_norm, lambda_qk, subln, rpb, w_br_a, w_br_b, w_br_c,
           w_out, g_ffn, w_router, w_e_gate, w_e_up, w_e_down):
    cfg = Cfg(d_model=x_prompt.shape[-1],
              groups=(x_prompt.shape[:2], x_sample.shape[:2]),
              n_experts=w_router.shape[-1], d_ff=w_e_gate.shape[-1], depth=g_mix.shape[0])
    p = dict(g_mix=g_mix, w_in=w_in, b_gate=b_gate, qk_norm=qk_norm, lambda_qk=lambda_qk, subln=subln, rpb=rpb,
             w_br_a=w_br_a, w_br_b=w_br_b, w_br_c=w_br_c, w_out=w_out, g_ffn=g_ffn, w_router=w_router,
             w_e_gate=w_e_gate, w_e_up=w_e_up, w_e_down=w_e_down)
    return trunk((x_prompt, x_sample), p, cfg)
```

```python
import functools
import math
from typing import NamedTuple

import numpy as np
import jax
import jax.numpy as jnp
from jax import lax
from jax.experimental import pallas as pl
from jax.experimental.pallas import tpu as pltpu

HEAD = 128
MIXW = 1536
QKV_COLS = 9 * MIXW
A_GROUPS = ((64, 1), (256, 4), (1024, 16))
A_HPG = 4
B_HEADS = 6
C_HEADS = 12
GRID_W = 64
NA_ROWS = 8
NA_COLS = 16
C_ROWS_PER_BLOCK = 4
C_TQ = C_ROWS_PER_BLOCK * GRID_W
EPS = 1e-6
NEG = -1e30
LOG2E = math.log2(math.e)
Q_SCALE = HEAD ** -0.5 * LOG2E
ROPE_THETA = 10000.0
CAP_FACTOR = 2
SEL_CHUNK = 256
VMEM_LIMIT = 56 * 1024 * 1024

BF = jnp.bfloat16
F32 = jnp.float32


class Cfg(NamedTuple):
    d_model: int
    groups: tuple
    n_experts: int
    d_ff: int
    depth: int
    tm: int = 1024
    tn: int = 512
    tn_in: int = 768
    tq_a: int = 128
    tq_b: int = 1024
    sub_b: int = 256
    tr: int = 256
    tf: int = 256


def _params(*sem):
    return pltpu.CompilerParams(dimension_semantics=sem, vmem_limit_bytes=VMEM_LIMIT)


def _rms(x):
    return x * lax.rsqrt(jnp.mean(x * x, axis=-1, keepdims=True) + EPS)


def _dot_t(a, b):
    return lax.dot_general(a, b, (((1,), (1,)), ((), ())), preferred_element_type=F32)


def _row_parts(parts, tm):
    return [(off // tm, (off + x.shape[0]) // tm) for x, off in parts]


def _part_row(i, b0, b1):
    return jnp.clip(i - b0, 0, b1 - b0 - 1)


def _rmsnorm_kernel(*refs, bounds):
    xs, g_ref, o_ref = refs[:len(bounds)], refs[-2], refs[-1]
    if len(bounds) == 1:
        o_ref[...] = (_rms(xs[0][...]) * g_ref[...]).astype(o_ref.dtype)
        return
    i = pl.program_id(0)
    for x_ref, (b0, b1) in zip(xs, bounds):
        @pl.when((i >= b0) & (i < b1))
        def _():
            o_ref[...] = (_rms(x_ref[...]) * g_ref[...]).astype(o_ref.dtype)


def rmsnorm_bf16(parts, g, m, tm):
    d = parts[0][0].shape[1]
    bounds = _row_parts(parts, tm)
    return pl.pallas_call(
        functools.partial(_rmsnorm_kernel, bounds=bounds),
        grid=(m // tm,),
        in_specs=[pl.BlockSpec((tm, d), lambda i, b0=b0, b1=b1: (_part_row(i, b0, b1), 0)) for b0, b1 in bounds]
        + [pl.BlockSpec((1, d), lambda i: (0, 0))],
        out_specs=pl.BlockSpec((tm, d), lambda i: (i, 0)),
        out_shape=jax.ShapeDtypeStruct((m, d), BF),
        compiler_params=_params("parallel"),
        name="rmsnorm",
    )(*[x for x, _ in parts], g.reshape(1, d))


def _in_proj_kernel(h_ref, w_ref, nw_ref, b_ref, cos_ref, sin_ref, o_ref, *, tn):
    seg = (pl.program_id(1) * tn) // MIXW
    acc = jnp.dot(h_ref[...], w_ref[...].astype(BF), preferred_element_type=F32)
    heads = tn // HEAD

    def normed(h):
        sl = slice(h * HEAD, (h + 1) * HEAD)
        return _rms(acc[:, sl]) * nw_ref[:, sl]

    @pl.when((seg == 0) | (seg == 1) | (seg == 3) | (seg == 4))
    def _():
        for h in range(heads):
            y = normed(h)
            y = y * cos_ref[...] + pltpu.roll(y, HEAD // 2, 1) * sin_ref[...]
            o_ref[:, h * HEAD:(h + 1) * HEAD] = y.astype(o_ref.dtype)

    @pl.when((seg == 6) | (seg == 7))
    def _():
        for h in range(heads):
            o_ref[:, h * HEAD:(h + 1) * HEAD] = normed(h).astype(o_ref.dtype)

    @pl.when((seg == 2) | (seg == 5) | (seg == 8))
    def _():
        o_ref[...] = acc.astype(o_ref.dtype)

    @pl.when(seg >= 9)
    def _():
        o_ref[...] = (0.5 * jnp.tanh(0.5 * (acc + b_ref[...])) + 0.5).astype(o_ref.dtype)


def in_proj(h, w_in, layer, nw, bias, cos, sin_signed, cfg):
    m, d = h.shape
    ncol = w_in.shape[2]
    tm, tn = cfg.tm, cfg.tn_in
    return pl.pallas_call(
        functools.partial(_in_proj_kernel, tn=tn),
        grid=(m // tm, ncol // tn),
        in_specs=[
            pl.BlockSpec((tm, d), lambda i, j: (i, 0), pipeline_mode=pl.Buffered(1)),
            pl.BlockSpec((None, d, tn), lambda i, j: (layer, 0, j)),
            pl.BlockSpec((1, tn), lambda i, j: (0, j)),
            pl.BlockSpec((1, tn), lambda i, j: (0, j)),
            pl.BlockSpec((tm, HEAD), lambda i, j: (i, 0)),
            pl.BlockSpec((tm, HEAD), lambda i, j: (i, 0)),
        ],
        out_specs=pl.BlockSpec((tm, tn), lambda i, j: (i, j)),
        out_shape=jax.ShapeDtypeStruct((m, ncol), BF),
        compiler_params=_params("parallel", "arbitrary"),
        name="in_proj",
    )(h, w_in, nw, bias, cos, sin_signed)


def _attn_a_kernel(q_ref, k_ref, v_ref, other_rows_hbm, o_ref, *, tq, seq):
    del other_rows_hbm
    t0 = pl.program_id(1) * tq
    geo = []
    for radius, dil in A_GROUPS:
        width = min(tq + 2 * radius, seq)
        start = pl.multiple_of(jnp.clip(t0 - radius, 0, seq - width), 64)
        diff = (start - t0) + lax.broadcasted_iota(jnp.int32, (tq, width), 1) \
            - lax.broadcasted_iota(jnp.int32, (tq, width), 0)
        valid = (jnp.abs(diff) <= radius) & ((diff & (dil - 1)) == 0)
        geo.append((start, width, jnp.where(valid, 0.0, NEG)))
    for hh in range(A_HPG):
        scores = []
        for g, (start, width, mask) in enumerate(geo):
            col = (g * A_HPG + hh) * HEAD
            q = q_ref[:, col:col + HEAD]
            k = k_ref[pl.ds(start, width), col:col + HEAD]
            scores.append(_dot_t(q, k) + mask)
        mx = functools.reduce(jnp.maximum, [jnp.max(s, axis=-1, keepdims=True) for s in scores])
        den = jnp.zeros((tq, 1), F32)
        acc = jnp.zeros((tq, HEAD), F32)
        for g, (start, width, mask) in enumerate(geo):
            col = (g * A_HPG + hh) * HEAD
            p = jnp.exp2(scores[g] - mx)
            den = den + jnp.sum(p, axis=-1, keepdims=True)
            v = v_ref[pl.ds(start, width), col:col + HEAD]
            acc = acc + jnp.dot(p.astype(BF), v, preferred_element_type=F32)
        o_ref[:, hh * HEAD:(hh + 1) * HEAD] = (acc / den).astype(o_ref.dtype)


def attn_a(proj, out, goff, batch, seq, cfg):
    tq = cfg.tq_a
    m = proj.shape[0]
    rb0, sb0 = goff // tq, goff // seq
    nq = seq // tq
    whole = dict(pipeline_mode=pl.Buffered(1))
    return pl.pallas_call(
        functools.partial(_attn_a_kernel, tq=tq, seq=seq),
        grid=(batch, nq),
        in_specs=[
            pl.BlockSpec((tq, MIXW), lambda b, i: (rb0 + b * nq + i, 0)),
            pl.BlockSpec((seq, MIXW), lambda b, i: (sb0 + b, 1), **whole),
            pl.BlockSpec((seq, MIXW), lambda b, i: (sb0 + b, 2), **whole),
            pl.BlockSpec(memory_space=pl.ANY),
        ],
        out_specs=pl.BlockSpec((tq, A_HPG * HEAD), lambda b, i: (rb0 + b * nq + i, 0)),
        out_shape=jax.ShapeDtypeStruct((m, A_HPG * HEAD), BF),
        input_output_aliases={3: 0},
        compiler_params=_params("parallel", "arbitrary"),
        name="attn_a",
    )(proj, proj, proj, out)


def _attn_b_kernel(q_ref, k_ref, v_ref, lam_ref, g_ref, other_rows_hbm, o_ref, s_even, s_odd, *, lam_init, sub):
    del other_rows_hbm
    n = q_ref.shape[0] // sub
    lq = lam_ref[...]
    lam = (jnp.exp(jnp.sum(lq[0:1] * lq[1:2], keepdims=True))
           - jnp.exp(jnp.sum(lq[2:3] * lq[3:4], keepdims=True)) + lam_init)

    def rows_of(i):
        return pl.ds(pl.multiple_of(i * sub, sub), sub)

    def scores(i, dst):
        for c in range(2):
            dst[c] = _dot_t(q_ref[rows_of(i), c * HEAD:(c + 1) * HEAD], k_ref[:, c * HEAD:(c + 1) * HEAD])

    def finish(i, src):
        def unnormalised(c):
            s = src[c]
            p = jnp.exp2(s - jnp.max(s, axis=-1, keepdims=True))
            return p, 1.0 / jnp.sum(p, axis=-1, keepdims=True)

        p0, r0 = unnormalised(0)
        p1, r1 = unnormalised(1)
        a = (p0 * r0 - p1 * (lam * r1)).astype(BF)
        o = jnp.dot(a, v_ref[...], preferred_element_type=F32)
        o_ref[rows_of(i), :] = ((_rms(o) * g_ref[...]) * (1.0 - lam_init)).astype(o_ref.dtype)

    scores(0, s_even)

    def pair(j, carry):
        i = 2 * j
        scores(i + 1, s_odd)
        finish(i, s_even)
        scores(i + 2, s_even)
        finish(i + 1, s_odd)
        return carry

    lax.fori_loop(0, n // 2 - 1, pair, 0)
    scores(n - 1, s_odd)
    finish(n - 2, s_even)
    finish(n - 1, s_odd)


def attn_b(proj, out, lambda_qk, subln, lam_init, goff, batch, seq, cfg):
    tq = min(cfg.tq_b, seq)
    sub = cfg.sub_b
    assert tq % (2 * sub) == 0
    m = proj.shape[0]
    w = 2 * HEAD
    rb0, sb0 = goff // tq, goff // seq
    nq = seq // tq
    c0 = 3 * MIXW // w
    return pl.pallas_call(
        functools.partial(_attn_b_kernel, lam_init=lam_init, sub=sub),
        grid=(batch, B_HEADS, nq),
        in_specs=[
            pl.BlockSpec((tq, w), lambda b, h, i: (rb0 + b * nq + i, c0 + h)),
            pl.BlockSpec((seq, w), lambda b, h, i: (sb0 + b, c0 + B_HEADS + h)),
            pl.BlockSpec((seq, w), lambda b, h, i: (sb0 + b, c0 + 2 * B_HEADS + h)),
            pl.BlockSpec((4, HEAD), lambda b, h, i: (0, 0)),
            pl.BlockSpec((1, w), lambda b, h, i: (0, 0)),
            pl.BlockSpec(memory_space=pl.ANY),
        ],
        out_specs=pl.BlockSpec((tq, w), lambda b, h, i: (rb0 + b * nq + i, h)),
        out_shape=jax.ShapeDtypeStruct((m, MIXW), BF),
        scratch_shapes=[pltpu.VMEM((2, sub, seq), F32), pltpu.VMEM((2, sub, seq), F32)],
        input_output_aliases={5: 0},
        compiler_params=_params("parallel", "parallel", "arbitrary"),
        name="attn_b",
    )(proj, proj, proj, lambda_qk, subln.reshape(1, w), out)


def _natten_bias_index():
    j = np.arange(GRID_W)
    toeplitz = np.clip(j[None, :] - j[:, None], -(NA_COLS - 1), NA_COLS - 1) + NA_COLS - 1
    cs = np.clip(j - NA_COLS // 2, 0, GRID_W - NA_COLS)
    col_ok = (j[None, :] >= cs[:, None]) & (j[None, :] < cs[:, None] + NA_COLS)
    a = np.arange(C_ROWS_PER_BLOCK)
    kb = np.arange(3 * C_ROWS_PER_BLOCK)
    drow = (kb[None, :] - C_ROWS_PER_BLOCK) - a[:, None]
    kc = kb // C_ROWS_PER_BLOCK - 1
    row_ok = np.stack([
        np.broadcast_to((kc >= 0)[None, :], drow.shape),
        (drow >= -(NA_ROWS // 2)) & (drow < NA_ROWS - NA_ROWS // 2),
        np.broadcast_to((kc <= 0)[None, :], drow.shape),
    ])
    ok = row_ok[:, :, None, :, None] & col_ok[None, None, :, None, :]
    rpb_row = np.clip(drow + NA_ROWS - 1, 0, 2 * NA_ROWS - 2)
    return toeplitz.astype(np.int32), rpb_row.astype(np.int32), ok.reshape(3, C_TQ, 3 * C_TQ)


def natten_bias(rpb):
    toeplitz, rpb_row, ok = _natten_bias_index()
    t = rpb[:, :, toeplitz]
    v = t[:, rpb_row]
    v = v.transpose(0, 1, 3, 2, 4).reshape(C_HEADS, C_TQ, 3 * C_TQ)
    return jnp.where(ok[:, None], v[None] * LOG2E, NEG).astype(F32)


def _attn_c_kernel(q_ref, kp_ref, ko_ref, kn_ref, vp_ref, vo_ref, vn_ref, bias_ref, other_rows_hbm, o_ref):
    del other_rows_hbm
    heads = o_ref.shape[1] // HEAD
    for h in range(heads):
        sl = slice(h * HEAD, (h + 1) * HEAD)
        k = jnp.concatenate([kp_ref[:, sl], ko_ref[:, sl], kn_ref[:, sl]], axis=0)
        v = jnp.concatenate([vp_ref[:, sl], vo_ref[:, sl], vn_ref[:, sl]], axis=0)
        s = _dot_t(q_ref[:, sl], k) + bias_ref[h]
        p = jnp.exp2(s - jnp.max(s, axis=-1, keepdims=True))
        o = jnp.dot(p.astype(BF), v, preferred_element_type=F32) / jnp.sum(p, axis=-1, keepdims=True)
        o_ref[:, sl] = o.astype(o_ref.dtype)


def attn_c(proj, out, bias, goff, batch, seq, cfg):
    m = proj.shape[0]
    wb = 4 * HEAD
    nhg = MIXW // wb
    nb = seq // C_TQ
    assert nb >= 3
    rb0 = goff // C_TQ
    cq, ck, cv = 6 * nhg, 7 * nhg, 8 * nhg

    def rows(b, i):
        return rb0 + b * nb + i

    def case(i):
        return jnp.where(i == 0, 0, jnp.where(i == nb - 1, 2, 1))

    def spec(c0, shift):
        return pl.BlockSpec((C_TQ, wb), lambda b, i, g: (rows(b, jnp.clip(i + shift, 0, nb - 1)), c0 + g))

    return pl.pallas_call(
        _attn_c_kernel,
        grid=(batch, nb, nhg),
        in_specs=[
            spec(cq, 0), spec(ck, -1), spec(ck, 0), spec(ck, 1), spec(cv, -1), spec(cv, 0), spec(cv, 1),
            pl.BlockSpec((None, 4, C_TQ, 3 * C_TQ), lambda b, i, g: (case(i), g, 0, 0)),
            pl.BlockSpec(memory_space=pl.ANY),
        ],
        out_specs=pl.BlockSpec((C_TQ, wb), lambda b, i, g: (rows(b, i), g)),
        out_shape=jax.ShapeDtypeStruct((m, MIXW), BF),
        input_output_aliases={8: 0},
        compiler_params=_params("parallel", "parallel", "arbitrary"),
        name="attn_c",
    )(proj, proj, proj, proj, proj, proj, proj, bias, out)


def _merge_kernel(oa_ref, ob_ref, oc_ref, wa_ref, wb_ref, wc_ref, ga_ref, gb_ref, gc_ref, o_ref):
    def br(o, w, g):
        return g[...].astype(F32) * jnp.dot(o[...], w[...].astype(BF), preferred_element_type=F32)

    o_ref[...] = (br(oa_ref, wa_ref, ga_ref) + br(ob_ref, wb_ref, gb_ref) + br(oc_ref, wc_ref, gc_ref)
                  ).astype(o_ref.dtype)


def merge(oa, ob, oc, wa, wb, wc, layer, proj, cfg):
    m = oa.shape[0]
    d = wa.shape[2]
    tm, tn = cfg.tm, cfg.tn
    g0 = QKV_COLS // tn
    gs = d // tn

    def rowblk(width):
        return pl.BlockSpec((tm, width), lambda i, j: (i, 0))

    def wblk(k):
        return pl.BlockSpec((None, k, tn), lambda i, j: (layer, 0, j))

    def gate(n):
        return pl.BlockSpec((tm, tn), lambda i, j: (i, g0 + n * gs + j))

    return pl.pallas_call(
        _merge_kernel,
        grid=(m // tm, d // tn),
        in_specs=[rowblk(oa.shape[1]), rowblk(MIXW), rowblk(MIXW), wblk(wa.shape[1]), wblk(MIXW), wblk(MIXW),
                  gate(0), gate(1), gate(2)],
        out_specs=pl.BlockSpec((tm, tn), lambda i, j: (i, j)),
        out_shape=jax.ShapeDtypeStruct((m, d), BF),
        compiler_params=_params("parallel", "arbitrary"),
        name="merge",
    )(oa, ob, oc, wa, wb, wc, proj, proj, proj)


def _out_proj_kernel(m_ref, w_ref, *refs, bounds):
    xs, o_ref = refs[:len(bounds)], refs[-1]
    acc = jnp.dot(m_ref[...], w_ref[...].astype(BF), preferred_element_type=F32)
    if len(bounds) == 1:
        o_ref[...] = xs[0][...] + acc
        return
    i = pl.program_id(0)
    for x_ref, (b0, b1) in zip(xs, bounds):
        @pl.when((i >= b0) & (i < b1))
        def _():
            o_ref[...] = x_ref[...] + acc


def out_proj(merged, w_out, layer, x_parts, cfg):
    m, d = merged.shape
    tm, tn = cfg.tm, cfg.tn
    bounds = _row_parts(x_parts, tm)
    last = d // tn - 1

    def residual(b0, b1):
        return pl.BlockSpec((tm, tn), lambda i, j: (_part_row(i, b0, b1),
                                                    jnp.where(i < b0, 0, jnp.where(i >= b1, last, j))))

    return pl.pallas_call(
        functools.partial(_out_proj_kernel, bounds=bounds),
        grid=(m // tm, d // tn),
        in_specs=[
            pl.BlockSpec((tm, d), lambda i, j: (i, 0)),
            pl.BlockSpec((None, d, tn), lambda i, j: (layer, 0, j)),
        ] + [residual(b0, b1) for b0, b1 in bounds],
        out_specs=pl.BlockSpec((tm, tn), lambda i, j: (i, j)),
        out_shape=jax.ShapeDtypeStruct((m, d), F32),
        compiler_params=_params("parallel", "arbitrary"),
        name="out_proj",
    )(merged, w_out, *[x for x, _ in x_parts])


def _router_kernel(x_ref, g_ref, w_ref, o_ref, *, n_experts):
    def split(v):
        hi = v.astype(BF)
        return hi, (v - hi.astype(F32)).astype(BF)

    h_hi, h_lo = split(_rms(x_ref[...]) * g_ref[...])
    w_hi, w_lo = split(w_ref[...])
    mm = functools.partial(jnp.dot, preferred_element_type=F32)
    logits = mm(h_hi, w_hi) + (mm(h_hi, w_lo) + mm(h_lo, w_hi))
    lane = lax.broadcasted_iota(jnp.int32, logits.shape, 1)
    logits = jnp.where(lane < n_experts, logits, NEG)
    p = jnp.exp(logits - jnp.max(logits, axis=-1, keepdims=True))
    o_ref[...] = p / jnp.sum(p, axis=-1, keepdims=True)


def router(x, g, w_router, cfg):
    m, d = x.shape
    tm = min(512, cfg.tm)
    e = w_router.shape[1]
    wpad = jnp.zeros((d, HEAD), F32).at[:, :e].set(w_router)
    return pl.pallas_call(
        functools.partial(_router_kernel, n_experts=e),
        grid=(m // tm,),
        in_specs=[pl.BlockSpec((tm, d), lambda i: (i, 0)), pl.BlockSpec((1, d), lambda i: (0, 0)),
                  pl.BlockSpec((d, HEAD), lambda i: (0, 0))],
        out_specs=pl.BlockSpec((tm, HEAD), lambda i: (i, 0)),
        out_shape=jax.ShapeDtypeStruct((m, HEAD), F32),
        compiler_params=_params("parallel"),
        name="router",
    )(x, g.reshape(1, d), wpad)


def _prefix_incl(x01, upper):
    within = jnp.dot(x01.astype(BF), upper, preferred_element_type=F32)
    totals = jnp.broadcast_to(within[:, SEL_CHUNK - 1:SEL_CHUNK], within.shape)
    chunk = lax.broadcasted_iota(jnp.int32, within.shape, 0)
    before = jnp.zeros_like(within)
    for j in range(within.shape[0] - 1):
        before = before + jnp.where(chunk > j, totals[j:j + 1, :], 0.0)
    return within + before


def _topc_kernel(a_ref, tab_ref, idx_ref, gate_ref, *, cap, n_experts):
    a = a_ref[...]
    nch = a.shape[0]
    bits = pltpu.bitcast(a, jnp.int32)

    def count(mask):
        return jnp.sum(jnp.where(mask, 1.0, 0.0), keepdims=True)

    def keeps(cand):
        return jnp.where(count(bits >= cand) >= cap, 1, 0)

    def two_bits(i, t):
        shift = 28 - 2 * i
        digit = keeps(t | jnp.left_shift(jnp.int32(1), shift))
        for d in (2, 3):
            digit = digit + keeps(t | jnp.left_shift(jnp.int32(d), shift))
        return t | jnp.left_shift(digit, shift)

    top = jnp.left_shift(keeps(jnp.full((1, 1), 1 << 30, jnp.int32)), 30)
    thr = lax.fori_loop(0, 15, two_bits, top)
    r_i = lax.broadcasted_iota(jnp.int32, (SEL_CHUNK, SEL_CHUNK), 0)
    c_i = lax.broadcasted_iota(jnp.int32, (SEL_CHUNK, SEL_CHUNK), 1)
    upper = jnp.where(r_i <= c_i, 1.0, 0.0).astype(BF)

    gt = bits > thr
    eq = bits == thr
    room = cap - count(gt)
    eq_f = jnp.where(eq, 1.0, 0.0)
    eq_before = _prefix_incl(eq_f, upper) - eq_f
    sel = gt | (eq & (eq_before < room))
    rank = jnp.where(sel, _prefix_incl(jnp.where(sel, 1.0, 0.0), upper), 0.0)

    want = (lax.broadcasted_iota(jnp.int32, (cap, SEL_CHUNK), 0) + 1).astype(F32)
    acc = jnp.zeros((cap, HEAD), F32)
    for j in range(nch):
        hit = jnp.where(rank[j:j + 1, :] == want, 1.0, 0.0).astype(BF)
        acc = acc + jnp.dot(hit, tab_ref[j * SEL_CHUNK:(j + 1) * SEL_CHUNK, :], preferred_element_type=F32)
    lane = lax.broadcasted_iota(jnp.int32, acc.shape, 1)

    def pick(l):
        return jnp.sum(jnp.where(lane == l, acc, 0.0), axis=-1, keepdims=True)

    e = pl.program_id(0)
    gate_ref[...] = (pick(e) + pick(e + n_experts)) + pick(e + 2 * n_experts)
    idx_ref[...] = (pick(3 * n_experts + 1) * SEL_CHUNK + pick(3 * n_experts)).astype(jnp.int32)


def topc_table(aff):
    n, e = aff.shape
    assert 3 * e + 2 <= HEAD and n // SEL_CHUNK <= 256
    hi = aff.astype(BF)
    rest = aff - hi.astype(F32)
    mid = rest.astype(BF)
    lo = (rest - mid.astype(F32)).astype(BF)
    tok = jnp.arange(n, dtype=jnp.int32)
    pos = (tok % SEL_CHUNK).astype(BF)[:, None]
    chunk = (tok // SEL_CHUNK).astype(BF)[:, None]
    return jnp.concatenate([hi, mid, lo, pos, chunk, jnp.zeros((n, HEAD - 3 * e - 2), BF)], axis=1)


def topc(aff, cap):
    n, e = aff.shape
    nch = n // SEL_CHUNK
    return pl.pallas_call(
        functools.partial(_topc_kernel, cap=cap, n_experts=e),
        grid=(e,),
        in_specs=[pl.BlockSpec((None, nch, SEL_CHUNK), lambda i: (i, 0, 0)),
                  pl.BlockSpec((n, HEAD), lambda i: (0, 0))],
        out_specs=[pl.BlockSpec((None, cap, 1), lambda i: (i, 0, 0)), pl.BlockSpec((None, cap, 1), lambda i: (i, 0, 0))],
        out_shape=[jax.ShapeDtypeStruct((e, cap, 1), jnp.int32), jax.ShapeDtypeStruct((e, cap, 1), F32)],
        compiler_params=_params("parallel"),
        name="topc",
    )(aff.T.reshape(e, nch, SEL_CHUNK), topc_table(aff))


def _row_copy(src_hbm, dst, sem, tok, r):
    return pltpu.make_async_copy(src_hbm.at[pl.ds(tok, 1), :], dst.at[pl.ds(r, 1), :], sem)


ROW_LOOP_UNROLL = 8


def _gather_norm_kernel(idx_ref, x_hbm, g_ref, o_ref, buf, sem, *, tr):
    i = pl.program_id(0)
    slot = i % 2

    def rows(tile, s, go):
        def body(r, c):
            cp = _row_copy(x_hbm, buf.at[s], sem.at[s], idx_ref[tile * tr + r], r)
            cp.start() if go else cp.wait()
            return c
        lax.fori_loop(0, tr, body, 0, unroll=ROW_LOOP_UNROLL)

    @pl.when(i == 0)
    def _():
        rows(0, 0, True)

    @pl.when(i + 1 < pl.num_programs(0))
    def _():
        rows(i + 1, 1 - slot, True)

    rows(i, slot, False)
    o_ref[...] = (_rms(buf[slot]) * g_ref[...]).astype(o_ref.dtype)


def gather_norm(x, g, idx, cfg):
    m, d = x.shape
    r = idx.shape[0]
    tr = cfg.tr
    return pl.pallas_call(
        functools.partial(_gather_norm_kernel, tr=tr),
        grid_spec=pltpu.PrefetchScalarGridSpec(
            num_scalar_prefetch=1,
            grid=(r // tr,),
            in_specs=[pl.BlockSpec(memory_space=pl.ANY), pl.BlockSpec((1, d), lambda i, idx: (0, 0))],
            out_specs=pl.BlockSpec((tr, d), lambda i, idx: (i, 0)),
            scratch_shapes=[pltpu.VMEM((2, tr, d), F32), pltpu.SemaphoreType.DMA((2,))],
        ),
        out_shape=jax.ShapeDtypeStruct((r, d), BF),
        compiler_params=_params("arbitrary"),
        name="gather_norm",
    )(idx, x, g.reshape(1, d))


def _ffn_up_kernel(x_ref, wg_ref, wu_ref, o_ref):
    x = x_ref[...]
    gate = jnp.dot(x, wg_ref[...].astype(BF), preferred_element_type=F32)
    up = jnp.dot(x, wu_ref[...].astype(BF), preferred_element_type=F32)
    o_ref[...] = (jax.nn.silu(gate) * up).astype(o_ref.dtype)


def ffn_up(xe, w_gate, w_up, layer, cfg):
    _, e, d, f = w_gate.shape
    rows = xe.shape[0] // e
    tf = cfg.tf
    return pl.pallas_call(
        _ffn_up_kernel,
        grid=(e, f // tf),
        in_specs=[
            pl.BlockSpec((rows, d), lambda i, j: (i, 0), pipeline_mode=pl.Buffered(1)),
            pl.BlockSpec((None, None, d, tf), lambda i, j: (layer, i, 0, j)),
            pl.BlockSpec((None, None, d, tf), lambda i, j: (layer, i, 0, j)),
        ],
        out_specs=pl.BlockSpec((rows, tf), lambda i, j: (i, j)),
        out_shape=jax.ShapeDtypeStruct((xe.shape[0], f), BF),
        compiler_params=_params("parallel", "arbitrary"),
        name="ffn_up",
    )(xe, w_gate, w_up)


def _ffn_down_kernel(h_ref, w_ref, g_ref, o_ref):
    o_ref[...] = jnp.dot(h_ref[...], w_ref[...].astype(BF), preferred_element_type=F32) * g_ref[...]


def ffn_down(hid, w_down, layer, gate, cfg):
    _, e, f, d = w_down.shape
    rows = hid.shape[0] // e
    tn = cfg.tn
    return pl.pallas_call(
        _ffn_down_kernel,
        grid=(e, d // tn),
        in_specs=[
            pl.BlockSpec((rows, f), lambda i, j: (i, 0)),
            pl.BlockSpec((None, None, f, tn), lambda i, j: (layer, i, 0, j)),
            pl.BlockSpec((rows, 1), lambda i, j: (i, 0)),
        ],
        out_specs=pl.BlockSpec((rows, tn), lambda i, j: (i, j)),
        out_shape=jax.ShapeDtypeStruct((hid.shape[0], d), F32),
        compiler_params=_params("parallel", "arbitrary"),
        name="ffn_down",
    )(hid, w_down, gate)


def _scatter_add_kernel(idx_ref, y_ref, x_hbm, o_hbm, buf, sem, *, tr):
    del x_hbm
    base = pl.program_id(0) * tr

    half = tr // 2
    store_sem = sem.at[2]

    def fetch(h, r):
        return pltpu.make_async_copy(o_hbm.at[pl.ds(idx_ref[base + r], 1), :], buf.at[pl.ds(r, 1), :], sem.at[h])

    def store(r):
        return pltpu.make_async_copy(buf.at[pl.ds(r, 1), :], o_hbm.at[pl.ds(idx_ref[base + r], 1), :], store_sem)

    def each(h, fn):
        def body(r, c):
            fn(r)
            return c
        lax.fori_loop(h * half, (h + 1) * half, body, 0, unroll=ROW_LOOP_UNROLL)

    for h in range(2):
        each(h, lambda r: fetch(h, r).start())
    for h in range(2):
        each(h, lambda r: fetch(h, r).wait())
        rows = pl.ds(h * half, half)
        buf[rows, :] = buf[rows, :] + y_ref[rows, :]
        each(h, lambda r: store(r).start())
    for h in range(2):
        each(h, lambda r: store(r).wait())


def scatter_add(x, ye, idx, cfg):
    m, d = x.shape
    r = idx.shape[0]
    tr = cfg.tr
    return pl.pallas_call(
        functools.partial(_scatter_add_kernel, tr=tr),
        grid_spec=pltpu.PrefetchScalarGridSpec(
            num_scalar_prefetch=1,
            grid=(r // tr,),
            in_specs=[pl.BlockSpec((tr, d), lambda i, idx: (i, 0)), pl.BlockSpec(memory_space=pl.ANY)],
            out_specs=pl.BlockSpec(memory_space=pl.ANY),
            scratch_shapes=[pltpu.VMEM((tr, d), F32), pltpu.SemaphoreType.DMA((3,))],
        ),
        out_shape=jax.ShapeDtypeStruct((m, d), F32),
        input_output_aliases={2: 0},
        compiler_params=_params("arbitrary"),
        name="scatter_add",
    )(idx, ye, x)


def _rope_tables(cfg):
    inv = 1.0 / (ROPE_THETA ** (jnp.arange(0, HEAD, 2, dtype=F32) / HEAD))
    cos, sin = [], []
    sign = jnp.where(jnp.arange(HEAD) < HEAD // 2, -1.0, 1.0).astype(F32)
    for batch, seq in cfg.groups:
        ang = jnp.arange(seq, dtype=F32)[:, None] * inv[None, :]
        ang = jnp.concatenate([ang, ang], axis=-1)
        cos.append(jnp.tile(jnp.cos(ang), (batch, 1)))
        sin.append(jnp.tile(jnp.sin(ang) * sign, (batch, 1)))
    return jnp.concatenate(cos, axis=0), jnp.concatenate(sin, axis=0)


def _group_offsets(cfg):
    offs, o = [], 0
    for batch, seq in cfg.groups:
        offs.append(o)
        o += batch * seq
    return offs, o


def _mixer(x_parts, l, cos, sin, p, cfg):
    d = cfg.d_model
    offs, m = _group_offsets(cfg)
    ones = jnp.ones((MIXW,), F32)
    heads = MIXW // HEAD
    qk = p["qk_norm"][l].astype(F32)
    nw = jnp.concatenate([
        jnp.tile(qk[0] * Q_SCALE, heads), jnp.tile(qk[1], heads), ones,
        jnp.tile(qk[2] * Q_SCALE, heads), jnp.tile(qk[3], heads), ones,
        jnp.tile(qk[4] * Q_SCALE, heads), jnp.tile(qk[5], heads), ones,
        jnp.ones((3 * d,), F32)]).reshape(1, -1)
    bias = jnp.concatenate([jnp.zeros((QKV_COLS,), F32), p["b_gate"][l].astype(F32)]).reshape(1, -1)

    h = rmsnorm_bf16(x_parts, p["g_mix"][l], m, min(512, cfg.tm))
    proj = in_proj(h, p["w_in"], l, nw, bias, cos, sin, cfg)

    lam_init = 0.8 - 0.6 * math.exp(-0.3 * l)
    cbias = natten_bias(p["rpb"][l].astype(F32))
    oa = jnp.zeros((m, A_HPG * HEAD), BF)
    ob = jnp.zeros((m, MIXW), BF)
    oc = jnp.zeros((m, MIXW), BF)
    for (batch, seq), goff in zip(cfg.groups, offs):
        oa = attn_a(proj, oa, goff, batch, seq, cfg)
        ob = attn_b(proj, ob, p["lambda_qk"][l].astype(F32), p["subln"][l].astype(F32), lam_init, goff, batch, seq, cfg)
        oc = attn_c(proj, oc, cbias, goff, batch, seq, cfg)
    merged = merge(oa, ob, oc, p["w_br_a"], p["w_br_b"], p["w_br_c"], l, proj, cfg)
    return out_proj(merged, p["w_out"], l, x_parts, cfg)


def _moe(x, l, p, cfg):
    e = cfg.n_experts
    offs, m = _group_offsets(cfg)
    aff = router(x, p["g_ffn"][l], p["w_router"][l], cfg)[:, :e]
    idx, gate = [], []
    for (batch, seq), goff in zip(cfg.groups, offs):
        n = batch * seq
        cap = CAP_FACTOR * n // e
        i_g, g_g = topc(aff[goff:goff + n], cap)
        idx.append(i_g.reshape(e, cap) + goff)
        gate.append(g_g.reshape(e, cap))
    idx = jnp.concatenate(idx, axis=1).reshape(-1)
    gate = jnp.concatenate(gate, axis=1).reshape(-1, 1)
    xe = gather_norm(x, p["g_ffn"][l], idx, cfg)
    hid = ffn_up(xe, p["w_e_gate"], p["w_e_up"], l, cfg)
    ye = ffn_down(hid, p["w_e_down"], l, gate, cfg)
    return scatter_add(x, ye, idx, cfg)


def trunk(xs, p, cfg):
    d = cfg.d_model
    offs, _ = _group_offsets(cfg)
    x_parts = [(a.reshape(-1, d), o) for a, o in zip(xs, offs)]
    cos, sin = _rope_tables(cfg)
    for l in range(cfg.depth):
        x = _mixer(x_parts, l, cos, sin, p, cfg)
        x = _moe(x, l, p, cfg)
        x_parts = [(x, 0)]
    return tuple(x[o:o + b * s].reshape(b, s, d) for (b, s), o in zip(cfg.groups, offs))


def kernel(x_prompt, x_sample, g_mix, w_in, b_gate, qk_norm, lambda_qk, subln, rpb, w_br_a, w_br_b, w_br_c,
           w_out, g_ffn, w_router, w_e_gate, w_e_up, w_e_down):
    cfg = Cfg(d_model=x_prompt.shape[-1],
              groups=(x_prompt.shape[:2], x_sample.shape[:2]),
              n_experts=w_router.shape[-1], d_ff=w_e_gate.shape[-1], depth=g_mix.shape[0])
    p = dict(g_mix=g_mix, w_in=w_in, b_gate=b_gate, qk_norm=qk_norm, lambda_qk=lambda_qk, subln=subln, rpb=rpb,
             w_br_a=w_br_a, w_br_b=w_br_b, w_br_c=w_br_c, w_out=w_out, g_ffn=g_ffn, w_router=w_router,
             w_e_gate=w_e_gate, w_e_up=w_e_up, w_e_down=w_e_down)
    return trunk((x_prompt, x_sample), p, cfg)
```

```python
import functools
import math
from typing import NamedTuple

import numpy as np
import jax
import jax.numpy as jnp
from jax import lax
from jax.experimental import pallas as pl
from jax.experimental.pallas import tpu as pltpu

HEAD = 128
MIXW = 1536
QKV_COLS = 9 * MIXW
A_GROUPS = ((64, 1), (256, 4), (1024, 16))
A_HPG = 4
B_HEADS = 6
C_HEADS = 12
GRID_W = 64
NA_ROWS = 8
NA_COLS = 16
C_ROWS_PER_BLOCK = 4
C_TQ = C_ROWS_PER_BLOCK * GRID_W
EPS = 1e-6
NEG = -1e30
LOG2E = math.log2(math.e)
Q_SCALE = HEAD ** -0.5 * LOG2E
ROPE_THETA = 10000.0
CAP_FACTOR = 2
SEL_CHUNK = 256
VMEM_LIMIT = 56 * 1024 * 1024

BF = jnp.bfloat16
F32 = jnp.float32


class Cfg(NamedTuple):
    d_model: int
    groups: tuple
    n_experts: int
    d_ff: int
    depth: int
    tm: int = 1024
    tn: int = 512
    tq_a: int = 128
    tq_b: int = 1024
    sub_b: int = 256
    tr: int = 256
    tf: int = 256


def _params(*sem):
    return pltpu.CompilerParams(dimension_semantics=sem, vmem_limit_bytes=VMEM_LIMIT)


def _rms(x):
    return x * lax.rsqrt(jnp.mean(x * x, axis=-1, keepdims=True) + EPS)


def _dot_t(a, b):
    return lax.dot_general(a, b, (((1,), (1,)), ((), ())), preferred_element_type=F32)


def _row_parts(parts, tm):
    return [(off // tm, (off + x.shape[0]) // tm) for x, off in parts]


def _part_row(i, b0, b1):
    return jnp.clip(i - b0, 0, b1 - b0 - 1)


def _rmsnorm_kernel(*refs, bounds):
    xs, g_ref, o_ref = refs[:len(bounds)], refs[-2], refs[-1]
    if len(bounds) == 1:
        o_ref[...] = (_rms(xs[0][...]) * g_ref[...]).astype(o_ref.dtype)
        return
    i = pl.program_id(0)
    for x_ref, (b0, b1) in zip(xs, bounds):
        @pl.when((i >= b0) & (i < b1))
        def _():
            o_ref[...] = (_rms(x_ref[...]) * g_ref[...]).astype(o_ref.dtype)


def rmsnorm_bf16(parts, g, m, tm):
    d = parts[0][0].shape[1]
    bounds = _row_parts(parts, tm)
    return pl.pallas_call(
        functools.partial(_rmsnorm_kernel, bounds=bounds),
        grid=(m // tm,),
        in_specs=[pl.BlockSpec((tm, d), lambda i, b0=b0, b1=b1: (_part_row(i, b0, b1), 0)) for b0, b1 in bounds]
        + [pl.BlockSpec((1, d), lambda i: (0, 0))],
        out_specs=pl.BlockSpec((tm, d), lambda i: (i, 0)),
        out_shape=jax.ShapeDtypeStruct((m, d), BF),
        compiler_params=_params("parallel"),
        name="rmsnorm",
    )(*[x for x, _ in parts], g.reshape(1, d))


def _in_proj_kernel(h_ref, w_ref, nw_ref, b_ref, cos_ref, sin_ref, o_ref, *, tn):
    seg = (pl.program_id(1) * tn) // MIXW
    acc = jnp.dot(h_ref[...], w_ref[...].astype(BF), preferred_element_type=F32)
    heads = tn // HEAD

    def normed(h):
        sl = slice(h * HEAD, (h + 1) * HEAD)
        return _rms(acc[:, sl]) * nw_ref[:, sl]

    @pl.when((seg == 0) | (seg == 1) | (seg == 3) | (seg == 4))
    def _():
        for h in range(heads):
            y = normed(h)
            y = y * cos_ref[...] + pltpu.roll(y, HEAD // 2, 1) * sin_ref[...]
            o_ref[:, h * HEAD:(h + 1) * HEAD] = y.astype(o_ref.dtype)

    @pl.when((seg == 6) | (seg == 7))
    def _():
        for h in range(heads):
            o_ref[:, h * HEAD:(h + 1) * HEAD] = normed(h).astype(o_ref.dtype)

    @pl.when((seg == 2) | (seg == 5) | (seg == 8))
    def _():
        o_ref[...] = acc.astype(o_ref.dtype)

    @pl.when(seg >= 9)
    def _():
        o_ref[...] = (0.5 * jnp.tanh(0.5 * (acc + b_ref[...])) + 0.5).astype(o_ref.dtype)


def in_proj(h, w_in, layer, nw, bias, cos, sin_signed, cfg):
    m, d = h.shape
    ncol = w_in.shape[2]
    tm, tn = cfg.tm, cfg.tn
    return pl.pallas_call(
        functools.partial(_in_proj_kernel, tn=tn),
        grid=(m // tm, ncol // tn),
        in_specs=[
            pl.BlockSpec((tm, d), lambda i, j: (i, 0)),
            pl.BlockSpec((None, d, tn), lambda i, j: (layer, 0, j)),
            pl.BlockSpec((1, tn), lambda i, j: (0, j)),
            pl.BlockSpec((1, tn), lambda i, j: (0, j)),
            pl.BlockSpec((tm, HEAD), lambda i, j: (i, 0)),
            pl.BlockSpec((tm, HEAD), lambda i, j: (i, 0)),
        ],
        out_specs=pl.BlockSpec((tm, tn), lambda i, j: (i, j)),
        out_shape=jax.ShapeDtypeStruct((m, ncol), BF),
        compiler_params=_params("parallel", "arbitrary"),
        name="in_proj",
    )(h, w_in, nw, bias, cos, sin_signed)


def _attn_a_kernel(q_ref, k_ref, v_ref, other_rows_hbm, o_ref, *, tq, seq):
    del other_rows_hbm
    t0 = pl.program_id(1) * tq
    geo = []
    for radius, dil in A_GROUPS:
        width = min(tq + 2 * radius, seq)
        start = pl.multiple_of(jnp.clip(t0 - radius, 0, seq - width), 64)
        diff = (start - t0) + lax.broadcasted_iota(jnp.int32, (tq, width), 1) \
            - lax.broadcasted_iota(jnp.int32, (tq, width), 0)
        valid = (jnp.abs(diff) <= radius) & ((diff & (dil - 1)) == 0)
        geo.append((start, width, jnp.where(valid, 0.0, NEG)))
    for hh in range(A_HPG):
        scores = []
        for g, (start, width, mask) in enumerate(geo):
            col = (g * A_HPG + hh) * HEAD
            q = q_ref[:, col:col + HEAD]
            k = k_ref[pl.ds(start, width), col:col + HEAD]
            scores.append(_dot_t(q, k) + mask)
        mx = functools.reduce(jnp.maximum, [jnp.max(s, axis=-1, keepdims=True) for s in scores])
        den = jnp.zeros((tq, 1), F32)
        acc = jnp.zeros((tq, HEAD), F32)
        for g, (start, width, mask) in enumerate(geo):
            col = (g * A_HPG + hh) * HEAD
            p = jnp.exp2(scores[g] - mx)
            den = den + jnp.sum(p, axis=-1, keepdims=True)
            v = v_ref[pl.ds(start, width), col:col + HEAD]
            acc = acc + jnp.dot(p.astype(BF), v, preferred_element_type=F32)
        o_ref[:, hh * HEAD:(hh + 1) * HEAD] = (acc / den).astype(o_ref.dtype)


def attn_a(proj, out, goff, batch, seq, cfg):
    tq = cfg.tq_a
    m = proj.shape[0]
    rb0, sb0 = goff // tq, goff // seq
    nq = seq // tq
    whole = dict(pipeline_mode=pl.Buffered(1))
    return pl.pallas_call(
        functools.partial(_attn_a_kernel, tq=tq, seq=seq),
        grid=(batch, nq),
        in_specs=[
            pl.BlockSpec((tq, MIXW), lambda b, i: (rb0 + b * nq + i, 0)),
            pl.BlockSpec((seq, MIXW), lambda b, i: (sb0 + b, 1), **whole),
            pl.BlockSpec((seq, MIXW), lambda b, i: (sb0 + b, 2), **whole),
            pl.BlockSpec(memory_space=pl.ANY),
        ],
        out_specs=pl.BlockSpec((tq, A_HPG * HEAD), lambda b, i: (rb0 + b * nq + i, 0)),
        out_shape=jax.ShapeDtypeStruct((m, A_HPG * HEAD), BF),
        input_output_aliases={3: 0},
        compiler_params=_params("parallel", "arbitrary"),
        name="attn_a",
    )(proj, proj, proj, out)


def _attn_b_kernel(q_ref, k_ref, v_ref, lam_ref, g_ref, other_rows_hbm, o_ref, s_even, s_odd, *, lam_init, sub):
    del other_rows_hbm
    n = q_ref.shape[0] // sub
    lq = lam_ref[...]
    lam = (jnp.exp(jnp.sum(lq[0:1] * lq[1:2], keepdims=True))
           - jnp.exp(jnp.sum(lq[2:3] * lq[3:4], keepdims=True)) + lam_init)

    def rows_of(i):
        return pl.ds(pl.multiple_of(i * sub, sub), sub)

    def scores(i, dst):
        for c in range(2):
            dst[c] = _dot_t(q_ref[rows_of(i), c * HEAD:(c + 1) * HEAD], k_ref[:, c * HEAD:(c + 1) * HEAD])

    def finish(i, src):
        def unnormalised(c):
            s = src[c]
            p = jnp.exp2(s - jnp.max(s, axis=-1, keepdims=True))
            return p, 1.0 / jnp.sum(p, axis=-1, keepdims=True)

        p0, r0 = unnormalised(0)
        p1, r1 = unnormalised(1)
        a = (p0 * r0 - p1 * (lam * r1)).astype(BF)
        o = jnp.dot(a, v_ref[...], preferred_element_type=F32)
        o_ref[rows_of(i), :] = ((_rms(o) * g_ref[...]) * (1.0 - lam_init)).astype(o_ref.dtype)

    scores(0, s_even)

    def pair(j, carry):
        i = 2 * j
        scores(i + 1, s_odd)
        finish(i, s_even)
        scores(i + 2, s_even)
        finish(i + 1, s_odd)
        return carry

    lax.fori_loop(0, n // 2 - 1, pair, 0)
    scores(n - 1, s_odd)
    finish(n - 2, s_even)
    finish(n - 1, s_odd)


def attn_b(proj, out, lambda_qk, subln, lam_init, goff, batch, seq, cfg):
    tq = min(cfg.tq_b, seq)
    sub = cfg.sub_b
    assert tq % (2 * sub) == 0
    m = proj.shape[0]
    w = 2 * HEAD
    rb0, sb0 = goff // tq, goff // seq
    nq = seq // tq
    c0 = 3 * MIXW // w
    return pl.pallas_call(
        functools.partial(_attn_b_kernel, lam_init=lam_init, sub=sub),
        grid=(batch, B_HEADS, nq),
        in_specs=[
            pl.BlockSpec((tq, w), lambda b, h, i: (rb0 + b * nq + i, c0 + h)),
            pl.BlockSpec((seq, w), lambda b, h, i: (sb0 + b, c0 + B_HEADS + h)),
            pl.BlockSpec((seq, w), lambda b, h, i: (sb0 + b, c0 + 2 * B_HEADS + h)),
            pl.BlockSpec((4, HEAD), lambda b, h, i: (0, 0)),
            pl.BlockSpec((1, w), lambda b, h, i: (0, 0)),
            pl.BlockSpec(memory_space=pl.ANY),
        ],
        out_specs=pl.BlockSpec((tq, w), lambda b, h, i: (rb0 + b * nq + i, h)),
        out_shape=jax.ShapeDtypeStruct((m, MIXW), BF),
        scratch_shapes=[pltpu.VMEM((2, sub, seq), F32), pltpu.VMEM((2, sub, seq), F32)],
        input_output_aliases={5: 0},
        compiler_params=_params("parallel", "parallel", "arbitrary"),
        name="attn_b",
    )(proj, proj, proj, lambda_qk, subln.reshape(1, w), out)


def _natten_bias_index():
    j = np.arange(GRID_W)
    toeplitz = np.clip(j[None, :] - j[:, None], -(NA_COLS - 1), NA_COLS - 1) + NA_COLS - 1
    cs = np.clip(j - NA_COLS // 2, 0, GRID_W - NA_COLS)
    col_ok = (j[None, :] >= cs[:, None]) & (j[None, :] < cs[:, None] + NA_COLS)
    a = np.arange(C_ROWS_PER_BLOCK)
    kb = np.arange(3 * C_ROWS_PER_BLOCK)
    drow = (kb[None, :] - C_ROWS_PER_BLOCK) - a[:, None]
    kc = kb // C_ROWS_PER_BLOCK - 1
    row_ok = np.stack([
        np.broadcast_to((kc >= 0)[None, :], drow.shape),
        (drow >= -(NA_ROWS // 2)) & (drow < NA_ROWS - NA_ROWS // 2),
        np.broadcast_to((kc <= 0)[None, :], drow.shape),
    ])
    ok = row_ok[:, :, None, :, None] & col_ok[None, None, :, None, :]
    rpb_row = np.clip(drow + NA_ROWS - 1, 0, 2 * NA_ROWS - 2)
    return toeplitz.astype(np.int32), rpb_row.astype(np.int32), ok.reshape(3, C_TQ, 3 * C_TQ)


def natten_bias(rpb):
    toeplitz, rpb_row, ok = _natten_bias_index()
    t = rpb[:, :, toeplitz]
    v = t[:, rpb_row]
    v = v.transpose(0, 1, 3, 2, 4).reshape(C_HEADS, C_TQ, 3 * C_TQ)
    return jnp.where(ok[:, None], v[None] * LOG2E, NEG).astype(F32)


def _attn_c_kernel(q_ref, kp_ref, ko_ref, kn_ref, vp_ref, vo_ref, vn_ref, bias_ref, other_rows_hbm, o_ref):
    del other_rows_hbm
    heads = o_ref.shape[1] // HEAD
    for h in range(heads):
        sl = slice(h * HEAD, (h + 1) * HEAD)
        k = jnp.concatenate([kp_ref[:, sl], ko_ref[:, sl], kn_ref[:, sl]], axis=0)
        v = jnp.concatenate([vp_ref[:, sl], vo_ref[:, sl], vn_ref[:, sl]], axis=0)
        s = _dot_t(q_ref[:, sl], k) + bias_ref[h]
        p = jnp.exp2(s - jnp.max(s, axis=-1, keepdims=True))
        o = jnp.dot(p.astype(BF), v, preferred_element_type=F32) / jnp.sum(p, axis=-1, keepdims=True)
        o_ref[:, sl] = o.astype(o_ref.dtype)


def attn_c(proj, out, bias, goff, batch, seq, cfg):
    m = proj.shape[0]
    wb = 4 * HEAD
    nhg = MIXW // wb
    nb = seq // C_TQ
    assert nb >= 3
    rb0 = goff // C_TQ
    cq, ck, cv = 6 * nhg, 7 * nhg, 8 * nhg

    def rows(b, i):
        return rb0 + b * nb + i

    def case(i):
        return jnp.where(i == 0, 0, jnp.where(i == nb - 1, 2, 1))

    def spec(c0, shift):
        return pl.BlockSpec((C_TQ, wb), lambda b, i, g: (rows(b, jnp.clip(i + shift, 0, nb - 1)), c0 + g))

    return pl.pallas_call(
        _attn_c_kernel,
        grid=(batch, nb, nhg),
        in_specs=[
            spec(cq, 0), spec(ck, -1), spec(ck, 0), spec(ck, 1), spec(cv, -1), spec(cv, 0), spec(cv, 1),
            pl.BlockSpec((None, 4, C_TQ, 3 * C_TQ), lambda b, i, g: (case(i), g, 0, 0)),
            pl.BlockSpec(memory_space=pl.ANY),
        ],
        out_specs=pl.BlockSpec((C_TQ, wb), lambda b, i, g: (rows(b, i), g)),
        out_shape=jax.ShapeDtypeStruct((m, MIXW), BF),
        input_output_aliases={8: 0},
        compiler_params=_params("parallel", "parallel", "arbitrary"),
        name="attn_c",
    )(proj, proj, proj, proj, proj, proj, proj, bias, out)


def _merge_kernel(oa_ref, ob_ref, oc_ref, wa_ref, wb_ref, wc_ref, ga_ref, gb_ref, gc_ref, o_ref):
    def br(o, w, g):
        return g[...].astype(F32) * jnp.dot(o[...], w[...].astype(BF), preferred_element_type=F32)

    o_ref[...] = (br(oa_ref, wa_ref, ga_ref) + br(ob_ref, wb_ref, gb_ref) + br(oc_ref, wc_ref, gc_ref)
                  ).astype(o_ref.dtype)


def merge(oa, ob, oc, wa, wb, wc, layer, proj, cfg):
    m = oa.shape[0]
    d = wa.shape[2]
    tm, tn = cfg.tm, cfg.tn
    g0 = QKV_COLS // tn
    gs = d // tn

    def rowblk(width):
        return pl.BlockSpec((tm, width), lambda i, j: (i, 0))

    def wblk(k):
        return pl.BlockSpec((None, k, tn), lambda i, j: (layer, 0, j))

    def gate(n):
        return pl.BlockSpec((tm, tn), lambda i, j: (i, g0 + n * gs + j))

    return pl.pallas_call(
        _merge_kernel,
        grid=(m // tm, d // tn),
        in_specs=[rowblk(oa.shape[1]), rowblk(MIXW), rowblk(MIXW), wblk(wa.shape[1]), wblk(MIXW), wblk(MIXW),
                  gate(0), gate(1), gate(2)],
        out_specs=pl.BlockSpec((tm, tn), lambda i, j: (i, j)),
        out_shape=jax.ShapeDtypeStruct((m, d), BF),
        compiler_params=_params("parallel", "arbitrary"),
        name="merge",
    )(oa, ob, oc, wa, wb, wc, proj, proj, proj)


def _out_proj_kernel(m_ref, w_ref, *refs, bounds):
    xs, o_ref = refs[:len(bounds)], refs[-1]
    acc = jnp.dot(m_ref[...], w_ref[...].astype(BF), preferred_element_type=F32)
    if len(bounds) == 1:
        o_ref[...] = xs[0][...] + acc
        return
    i = pl.program_id(0)
    for x_ref, (b0, b1) in zip(xs, bounds):
        @pl.when((i >= b0) & (i < b1))
        def _():
            o_ref[...] = x_ref[...] + acc


def out_proj(merged, w_out, layer, x_parts, cfg):
    m, d = merged.shape
    tm, tn = cfg.tm, cfg.tn
    bounds = _row_parts(x_parts, tm)
    last = d // tn - 1

    def residual(b0, b1):
        return pl.BlockSpec((tm, tn), lambda i, j: (_part_row(i, b0, b1),
                                                    jnp.where(i < b0, 0, jnp.where(i >= b1, last, j))))

    return pl.pallas_call(
        functools.partial(_out_proj_kernel, bounds=bounds),
        grid=(m // tm, d // tn),
        in_specs=[
            pl.BlockSpec((tm, d), lambda i, j: (i, 0)),
            pl.BlockSpec((None, d, tn), lambda i, j: (layer, 0, j)),
        ] + [residual(b0, b1) for b0, b1 in bounds],
        out_specs=pl.BlockSpec((tm, tn), lambda i, j: (i, j)),
        out_shape=jax.ShapeDtypeStruct((m, d), F32),
        compiler_params=_params("parallel", "arbitrary"),
        name="out_proj",
    )(merged, w_out, *[x for x, _ in x_parts])


def _router_kernel(x_ref, g_ref, w_ref, o_ref, *, n_experts):
    def split(v):
        hi = v.astype(BF)
        return hi, (v - hi.astype(F32)).astype(BF)

    h_hi, h_lo = split(_rms(x_ref[...]) * g_ref[...])
    w_hi, w_lo = split(w_ref[...])
    mm = functools.partial(jnp.dot, preferred_element_type=F32)
    logits = mm(h_hi, w_hi) + (mm(h_hi, w_lo) + mm(h_lo, w_hi))
    lane = lax.broadcasted_iota(jnp.int32, logits.shape, 1)
    logits = jnp.where(lane < n_experts, logits, NEG)
    p = jnp.exp(logits - jnp.max(logits, axis=-1, keepdims=True))
    o_ref[...] = p / jnp.sum(p, axis=-1, keepdims=True)


def router(x, g, w_router, cfg):
    m, d = x.shape
    tm = min(512, cfg.tm)
    e = w_router.shape[1]
    wpad = jnp.zeros((d, HEAD), F32).at[:, :e].set(w_router)
    return pl.pallas_call(
        functools.partial(_router_kernel, n_experts=e),
        grid=(m // tm,),
        in_specs=[pl.BlockSpec((tm, d), lambda i: (i, 0)), pl.BlockSpec((1, d), lambda i: (0, 0)),
                  pl.BlockSpec((d, HEAD), lambda i: (0, 0))],
        out_specs=pl.BlockSpec((tm, HEAD), lambda i: (i, 0)),
        out_shape=jax.ShapeDtypeStruct((m, HEAD), F32),
        compiler_params=_params("parallel"),
        name="router",
    )(x, g.reshape(1, d), wpad)


def _prefix_incl(x01, upper):
    within = jnp.dot(x01.astype(BF), upper, preferred_element_type=F32)
    totals = jnp.broadcast_to(within[:, SEL_CHUNK - 1:SEL_CHUNK], within.shape)
    chunk = lax.broadcasted_iota(jnp.int32, within.shape, 0)
    before = jnp.zeros_like(within)
    for j in range(within.shape[0] - 1):
        before = before + jnp.where(chunk > j, totals[j:j + 1, :], 0.0)
    return within + before


def _topc_kernel(a_ref, tab_ref, idx_ref, gate_ref, *, cap, n_experts):
    a = a_ref[...]
    nch = a.shape[0]
    bits = pltpu.bitcast(a, jnp.int32)

    def count(mask):
        return jnp.sum(jnp.where(mask, 1.0, 0.0), keepdims=True)

    def keeps(cand):
        return jnp.where(count(bits >= cand) >= cap, 1, 0)

    def two_bits(i, t):
        shift = 28 - 2 * i
        digit = keeps(t | jnp.left_shift(jnp.int32(1), shift))
        for d in (2, 3):
            digit = digit + keeps(t | jnp.left_shift(jnp.int32(d), shift))
        return t | jnp.left_shift(digit, shift)

    top = jnp.left_shift(keeps(jnp.full((1, 1), 1 << 30, jnp.int32)), 30)
    thr = lax.fori_loop(0, 15, two_bits, top)
    r_i = lax.broadcasted_iota(jnp.int32, (SEL_CHUNK, SEL_CHUNK), 0)
    c_i = lax.broadcasted_iota(jnp.int32, (SEL_CHUNK, SEL_CHUNK), 1)
    upper = jnp.where(r_i <= c_i, 1.0, 0.0).astype(BF)

    gt = bits > thr
    eq = bits == thr
    room = cap - count(gt)
    eq_f = jnp.where(eq, 1.0, 0.0)
    eq_before = _prefix_incl(eq_f, upper) - eq_f
    sel = gt | (eq & (eq_before < room))
    rank = jnp.where(sel, _prefix_incl(jnp.where(sel, 1.0, 0.0), upper), 0.0)

    want = (lax.broadcasted_iota(jnp.int32, (cap, SEL_CHUNK), 0) + 1).astype(F32)
    acc = jnp.zeros((cap, HEAD), F32)
    for j in range(nch):
        hit = jnp.where(rank[j:j + 1, :] == want, 1.0, 0.0).astype(BF)
        acc = acc + jnp.dot(hit, tab_ref[j * SEL_CHUNK:(j + 1) * SEL_CHUNK, :], preferred_element_type=F32)
    lane = lax.broadcasted_iota(jnp.int32, acc.shape, 1)

    def pick(l):
        return jnp.sum(jnp.where(lane == l, acc, 0.0), axis=-1, keepdims=True)

    e = pl.program_id(0)
    gate_ref[...] = (pick(e) + pick(e + n_experts)) + pick(e + 2 * n_experts)
    idx_ref[...] = (pick(3 * n_experts + 1) * SEL_CHUNK + pick(3 * n_experts)).astype(jnp.int32)


def topc_table(aff):
    n, e = aff.shape
    assert 3 * e + 2 <= HEAD and n // SEL_CHUNK <= 256
    hi = aff.astype(BF)
    rest = aff - hi.astype(F32)
    mid = rest.astype(BF)
    lo = (rest - mid.astype(F32)).astype(BF)
    tok = jnp.arange(n, dtype=jnp.int32)
    pos = (tok % SEL_CHUNK).astype(BF)[:, None]
    chunk = (tok // SEL_CHUNK).astype(BF)[:, None]
    return jnp.concatenate([hi, mid, lo, pos, chunk, jnp.zeros((n, HEAD - 3 * e - 2), BF)], axis=1)


def topc(aff, cap):
    n, e = aff.shape
    nch = n // SEL_CHUNK
    return pl.pallas_call(
        functools.partial(_topc_kernel, cap=cap, n_experts=e),
        grid=(e,),
        in_specs=[pl.BlockSpec((None, nch, SEL_CHUNK), lambda i: (i, 0, 0)),
                  pl.BlockSpec((n, HEAD), lambda i: (0, 0))],
        out_specs=[pl.BlockSpec((None, cap, 1), lambda i: (i, 0, 0)), pl.BlockSpec((None, cap, 1), lambda i: (i, 0, 0))],
        out_shape=[jax.ShapeDtypeStruct((e, cap, 1), jnp.int32), jax.ShapeDtypeStruct((e, cap, 1), F32)],
        compiler_params=_params("parallel"),
        name="topc",
    )(aff.T.reshape(e, nch, SEL_CHUNK), topc_table(aff))


def _row_copy(src_hbm, dst, sem, tok, r):
    return pltpu.make_async_copy(src_hbm.at[pl.ds(tok, 1), :], dst.at[pl.ds(r, 1), :], sem)


ROW_LOOP_UNROLL = 8


def _gather_norm_kernel(idx_ref, x_hbm, g_ref, o_ref, buf, sem, *, tr):
    i = pl.program_id(0)
    slot = i % 2

    def rows(tile, s, go):
        def body(r, c):
            cp = _row_copy(x_hbm, buf.at[s], sem.at[s], idx_ref[tile * tr + r], r)
            cp.start() if go else cp.wait()
            return c
        lax.fori_loop(0, tr, body, 0, unroll=ROW_LOOP_UNROLL)

    @pl.when(i == 0)
    def _():
        rows(0, 0, True)

    @pl.when(i + 1 < pl.num_programs(0))
    def _():
        rows(i + 1, 1 - slot, True)

    rows(i, slot, False)
    o_ref[...] = (_rms(buf[slot]) * g_ref[...]).astype(o_ref.dtype)


def gather_norm(x, g, idx, cfg):
    m, d = x.shape
    r = idx.shape[0]
    tr = cfg.tr
    return pl.pallas_call(
        functools.partial(_gather_norm_kernel, tr=tr),
        grid_spec=pltpu.PrefetchScalarGridSpec(
            num_scalar_prefetch=1,
            grid=(r // tr,),
            in_specs=[pl.BlockSpec(memory_space=pl.ANY), pl.BlockSpec((1, d), lambda i, idx: (0, 0))],
            out_specs=pl.BlockSpec((tr, d), lambda i, idx: (i, 0)),
            scratch_shapes=[pltpu.VMEM((2, tr, d), F32), pltpu.SemaphoreType.DMA((2,))],
        ),
        out_shape=jax.ShapeDtypeStruct((r, d), BF),
        compiler_params=_params("arbitrary"),
        name="gather_norm",
    )(idx, x, g.reshape(1, d))


def _ffn_up_kernel(x_ref, wg_ref, wu_ref, o_ref):
    x = x_ref[...]
    gate = jnp.dot(x, wg_ref[...].astype(BF), preferred_element_type=F32)
    up = jnp.dot(x, wu_ref[...].astype(BF), preferred_element_type=F32)
    o_ref[...] = (jax.nn.silu(gate) * up).astype(o_ref.dtype)


def ffn_up(xe, w_gate, w_up, layer, cfg):
    _, e, d, f = w_gate.shape
    rows = xe.shape[0] // e
    tf = cfg.tf
    return pl.pallas_call(
        _ffn_up_kernel,
        grid=(e, f // tf),
        in_specs=[
            pl.BlockSpec((rows, d), lambda i, j: (i, 0), pipeline_mode=pl.Buffered(1)),
            pl.BlockSpec((None, None, d, tf), lambda i, j: (layer, i, 0, j)),
            pl.BlockSpec((None, None, d, tf), lambda i, j: (layer, i, 0, j)),
        ],
        out_specs=pl.BlockSpec((rows, tf), lambda i, j: (i, j)),
        out_shape=jax.ShapeDtypeStruct((xe.shape[0], f), BF),
        compiler_params=_params("parallel", "arbitrary"),
        name="ffn_up",
    )(xe, w_gate, w_up)


def _ffn_down_kernel(h_ref, w_ref, g_ref, o_ref):
    o_ref[...] = jnp.dot(h_ref[...], w_ref[...].astype(BF), preferred_element_type=F32) * g_ref[...]


def ffn_down(hid, w_down, layer, gate, cfg):
    _, e, f, d = w_down.shape
    rows = hid.shape[0] // e
    tn = cfg.tn
    return pl.pallas_call(
        _ffn_down_kernel,
        grid=(e, d // tn),
        in_specs=[
            pl.BlockSpec((rows, f), lambda i, j: (i, 0)),
            pl.BlockSpec((None, None, f, tn), lambda i, j: (layer, i, 0, j)),
            pl.BlockSpec((rows, 1), lambda i, j: (i, 0)),
        ],
        out_specs=pl.BlockSpec((rows, tn), lambda i, j: (i, j)),
        out_shape=jax.ShapeDtypeStruct((hid.shape[0], d), F32),
        compiler_params=_params("parallel", "arbitrary"),
        name="ffn_down",
    )(hid, w_down, gate)


def _scatter_add_kernel(idx_ref, y_ref, x_hbm, o_hbm, buf, sem, *, tr, tiles_per_expert):
    del x_hbm
    i = pl.program_id(0)
    slot = i % 2
    pos = i % tiles_per_expert
    first = pos == 0
    last = pos == tiles_per_expert - 1

    def fetch(tile, s, r):
        tok = idx_ref[tile * tr + r]
        return pltpu.make_async_copy(o_hbm.at[pl.ds(tok, 1), :], buf.at[s, pl.ds(r, 1), :], sem.at[s])

    def store(tile, s, r):
        tok = idx_ref[tile * tr + r]
        return pltpu.make_async_copy(buf.at[s, pl.ds(r, 1), :], o_hbm.at[pl.ds(tok, 1), :], sem.at[2 + s])

    def each(fn):
        def body(r, c):
            fn(r)
            return c
        lax.fori_loop(0, tr, body, 0, unroll=ROW_LOOP_UNROLL)

    @pl.when(first)
    def _():
        each(lambda r: fetch(i, slot, r).start())

    @pl.when(jnp.logical_not(first))
    def _():
        each(lambda r: store(i - 1, 1 - slot, r).wait())

    @pl.when(jnp.logical_not(last))
    def _():
        each(lambda r: fetch(i + 1, 1 - slot, r).start())

    each(lambda r: fetch(i, slot, r).wait())
    buf[slot] = buf[slot] + y_ref[...]
    each(lambda r: store(i, slot, r).start())

    @pl.when(last)
    def _():
        each(lambda r: store(i, slot, r).wait())


def scatter_add(x, ye, idx, cfg):
    m, d = x.shape
    r = idx.shape[0]
    tr = cfg.tr
    assert (r // cfg.n_experts) % tr == 0
    return pl.pallas_call(
        functools.partial(_scatter_add_kernel, tr=tr, tiles_per_expert=r // cfg.n_experts // tr),
        grid_spec=pltpu.PrefetchScalarGridSpec(
            num_scalar_prefetch=1,
            grid=(r // tr,),
            in_specs=[pl.BlockSpec((tr, d), lambda i, idx: (i, 0)), pl.BlockSpec(memory_space=pl.ANY)],
            out_specs=pl.BlockSpec(memory_space=pl.ANY),
            scratch_shapes=[pltpu.VMEM((2, tr, d), F32), pltpu.SemaphoreType.DMA((4,))],
        ),
        out_shape=jax.ShapeDtypeStruct((m, d), F32),
        input_output_aliases={2: 0},
        compiler_params=_params("arbitrary"),
        name="scatter_add",
    )(idx, ye, x)


def _rope_tables(cfg):
    inv = 1.0 / (ROPE_THETA ** (jnp.arange(0, HEAD, 2, dtype=F32) / HEAD))
    cos, sin = [], []
    sign = jnp.where(jnp.arange(HEAD) < HEAD // 2, -1.0, 1.0).astype(F32)
    for batch, seq in cfg.groups:
        ang = jnp.arange(seq, dtype=F32)[:, None] * inv[None, :]
        ang = jnp.concatenate([ang, ang], axis=-1)
        cos.append(jnp.tile(jnp.cos(ang), (batch, 1)))
        sin.append(jnp.tile(jnp.sin(ang) * sign, (batch, 1)))
    return jnp.concatenate(cos, axis=0), jnp.concatenate(sin, axis=0)


def _group_offsets(cfg):
    offs, o = [], 0
    for batch, seq in cfg.groups:
        offs.append(o)
        o += batch * seq
    return offs, o


def _mixer(x_parts, l, cos, sin, p, cfg):
    d = cfg.d_model
    offs, m = _group_offsets(cfg)
    ones = jnp.ones((MIXW,), F32)
    heads = MIXW // HEAD
    qk = p["qk_norm"][l].astype(F32)
    nw = jnp.concatenate([
        jnp.tile(qk[0] * Q_SCALE, heads), jnp.tile(qk[1], heads), ones,
        jnp.tile(qk[2] * Q_SCALE, heads), jnp.tile(qk[3], heads), ones,
        jnp.tile(qk[4] * Q_SCALE, heads), jnp.tile(qk[5], heads), ones,
        jnp.ones((3 * d,), F32)]).reshape(1, -1)
    bias = jnp.concatenate([jnp.zeros((QKV_COLS,), F32), p["b_gate"][l].astype(F32)]).reshape(1, -1)

    h = rmsnorm_bf16(x_parts, p["g_mix"][l], m, min(512, cfg.tm))
    proj = in_proj(h, p["w_in"], l, nw, bias, cos, sin, cfg)

    lam_init = 0.8 - 0.6 * math.exp(-0.3 * l)
    cbias = natten_bias(p["rpb"][l].astype(F32))
    oa = jnp.zeros((m, A_HPG * HEAD), BF)
    ob = jnp.zeros((m, MIXW), BF)
    oc = jnp.zeros((m, MIXW), BF)
    for (batch, seq), goff in zip(cfg.groups, offs):
        oa = attn_a(proj, oa, goff, batch, seq, cfg)
        ob = attn_b(proj, ob, p["lambda_qk"][l].astype(F32), p["subln"][l].astype(F32), lam_init, goff, batch, seq, cfg)
        oc = attn_c(proj, oc, cbias, goff, batch, seq, cfg)
    merged = merge(oa, ob, oc, p["w_br_a"], p["w_br_b"], p["w_br_c"], l, proj, cfg)
    return out_proj(merged, p["w_out"], l, x_parts, cfg)


def _moe(x, l, p, cfg):
    e = cfg.n_experts
    offs, m = _group_offsets(cfg)
    aff = router(x, p["g_ffn"][l], p["w_router"][l], cfg)[:, :e]
    idx, gate = [], []
    for (batch, seq), goff in zip(cfg.groups, offs):
        n = batch * seq
        cap = CAP_FACTOR * n // e
        i_g, g_g = topc(aff[goff:goff + n], cap)
        idx.append(i_g.reshape(e, cap) + goff)
        gate.append(g_g.reshape(e, cap))
    idx = jnp.concatenate(idx, axis=1).reshape(-1)
    gate = jnp.concatenate(gate, axis=1).reshape(-1, 1)
    xe = gather_norm(x, p["g_ffn"][l], idx, cfg)
    hid = ffn_up(xe, p["w_e_gate"], p["w_e_up"], l, cfg)
    ye = ffn_down(hid, p["w_e_down"], l, gate, cfg)
    return scatter_add(x, ye, idx, cfg)


def trunk(xs, p, cfg):
    d = cfg.d_model
    offs, _ = _group_offsets(cfg)
    x_parts = [(a.reshape(-1, d), o) for a, o in zip(xs, offs)]
    cos, sin = _rope_tables(cfg)
    for l in range(cfg.depth):
        x = _mixer(x_parts, l, cos, sin, p, cfg)
        x = _moe(x, l, p, cfg)
        x_parts = [(x, 0)]
    return tuple(x[o:o + b * s].reshape(b, s, d) for (b, s), o in zip(cfg.groups, offs))


def kernel(x_prompt, x_sample, g_mix, w_in, b_gate, qk_norm, lambda_qk, subln, rpb, w_br_a, w_br_b, w_br_c,
           w_out, g_ffn, w_router, w_e_gate, w_e_up, w_e_down):
    cfg = Cfg(d_model=x_prompt.shape[-1],
              groups=(x_prompt.shape[:2], x_sample.shape[:2]),
              n_experts=w_router.shape[-1], d_ff=w_e_gate.shape[-1], depth=g_mix.shape[0])
    p = dict(g_mix=g_mix, w_in=w_in, b_gate=b_gate, qk_norm=qk_norm, lambda_qk=lambda_qk, subln=subln, rpb=rpb,
             w_br_a=w_br_a, w_br_b=w_br_b, w_br_c=w_br_c, w_out=w_out, g_ffn=g_ffn, w_router=w_router,
             w_e_gate=w_e_gate, w_e_up=w_e_up, w_e_down=w_e_down)
    return trunk((x_prompt, x_sample), p, cfg)
```

```python
import functools
import math
from typing import NamedTuple

import numpy as np
import jax
import jax.numpy as jnp
from jax import lax
from jax.experimental import pallas as pl
from jax.experimental.pallas import tpu as pltpu

HEAD = 128
MIXW = 1536
QKV_COLS = 9 * MIXW
A_GROUPS = ((64, 1), (256, 4), (1024, 16))
A_HPG = 4
B_HEADS = 6
C_HEADS = 12
GRID_W = 64
NA_ROWS = 8
NA_COLS = 16
C_ROWS_PER_BLOCK = 4
C_TQ = C_ROWS_PER_BLOCK * GRID_W
EPS = 1e-6
NEG = -1e30
LOG2E = math.log2(math.e)
Q_SCALE = HEAD ** -0.5 * LOG2E
ROPE_THETA = 10000.0
CAP_FACTOR = 2
SEL_CHUNK = 256
VMEM_LIMIT = 56 * 1024 * 1024

BF = jnp.bfloat16
F32 = jnp.float32


class Cfg(NamedTuple):
    d_model: int
    groups: tuple
    n_experts: int
    d_ff: int
    depth: int
    tm: int = 1024
    tn: int = 512
    tq_a: int = 128
    tq_b: int = 1024
    sub_b: int = 256
    tr: int = 256
    tf: int = 256


def _params(*sem):
    return pltpu.CompilerParams(dimension_semantics=sem, vmem_limit_bytes=VMEM_LIMIT)


def _rms(x):
    return x * lax.rsqrt(jnp.mean(x * x, axis=-1, keepdims=True) + EPS)


def _dot_t(a, b):
    return lax.dot_general(a, b, (((1,), (1,)), ((), ())), preferred_element_type=F32)


def _row_parts(parts, tm):
    return [(off // tm, (off + x.shape[0]) // tm) for x, off in parts]


def _part_row(i, b0, b1):
    return jnp.clip(i - b0, 0, b1 - b0 - 1)


def _rmsnorm_kernel(*refs, bounds):
    xs, g_ref, o_ref = refs[:len(bounds)], refs[-2], refs[-1]
    if len(bounds) == 1:
        o_ref[...] = (_rms(xs[0][...]) * g_ref[...]).astype(o_ref.dtype)
        return
    i = pl.program_id(0)
    for x_ref, (b0, b1) in zip(xs, bounds):
        @pl.when((i >= b0) & (i < b1))
        def _():
            o_ref[...] = (_rms(x_ref[...]) * g_ref[...]).astype(o_ref.dtype)


def rmsnorm_bf16(parts, g, m, tm):
    d = parts[0][0].shape[1]
    bounds = _row_parts(parts, tm)
    return pl.pallas_call(
        functools.partial(_rmsnorm_kernel, bounds=bounds),
        grid=(m // tm,),
        in_specs=[pl.BlockSpec((tm, d), lambda i, b0=b0, b1=b1: (_part_row(i, b0, b1), 0)) for b0, b1 in bounds]
        + [pl.BlockSpec((1, d), lambda i: (0, 0))],
        out_specs=pl.BlockSpec((tm, d), lambda i: (i, 0)),
        out_shape=jax.ShapeDtypeStruct((m, d), BF),
        compiler_params=_params("parallel"),
        name="rmsnorm",
    )(*[x for x, _ in parts], g.reshape(1, d))


def _in_proj_kernel(h_ref, w_ref, nw_ref, b_ref, cos_ref, sin_ref, o_ref, *, tn):
    seg = (pl.program_id(1) * tn) // MIXW
    acc = jnp.dot(h_ref[...], w_ref[...].astype(BF), preferred_element_type=F32)
    heads = tn // HEAD

    def normed(h):
        sl = slice(h * HEAD, (h + 1) * HEAD)
        return _rms(acc[:, sl]) * nw_ref[:, sl]

    @pl.when((seg == 0) | (seg == 1) | (seg == 3) | (seg == 4))
    def _():
        for h in range(heads):
            y = normed(h)
            y = y * cos_ref[...] + pltpu.roll(y, HEAD // 2, 1) * sin_ref[...]
            o_ref[:, h * HEAD:(h + 1) * HEAD] = y.astype(o_ref.dtype)

    @pl.when((seg == 6) | (seg == 7))
    def _():
        for h in range(heads):
            o_ref[:, h * HEAD:(h + 1) * HEAD] = normed(h).astype(o_ref.dtype)

    @pl.when((seg == 2) | (seg == 5) | (seg == 8))
    def _():
        o_ref[...] = acc.astype(o_ref.dtype)

    @pl.when(seg >= 9)
    def _():
        o_ref[...] = (0.5 * jnp.tanh(0.5 * (acc + b_ref[...])) + 0.5).astype(o_ref.dtype)


def in_proj(h, w_in, layer, nw, bias, cos, sin_signed, cfg):
    m, d = h.shape
    ncol = w_in.shape[2]
    tm, tn = cfg.tm, cfg.tn
    return pl.pallas_call(
        functools.partial(_in_proj_kernel, tn=tn),
        grid=(m // tm, ncol // tn),
        in_specs=[
            pl.BlockSpec((tm, d), lambda i, j: (i, 0)),
            pl.BlockSpec((None, d, tn), lambda i, j: (layer, 0, j)),
            pl.BlockSpec((1, tn), lambda i, j: (0, j)),
            pl.BlockSpec((1, tn), lambda i, j: (0, j)),
            pl.BlockSpec((tm, HEAD), lambda i, j: (i, 0)),
            pl.BlockSpec((tm, HEAD), lambda i, j: (i, 0)),
        ],
        out_specs=pl.BlockSpec((tm, tn), lambda i, j: (i, j)),
        out_shape=jax.ShapeDtypeStruct((m, ncol), BF),
        compiler_params=_params("parallel", "arbitrary"),
        name="in_proj",
    )(h, w_in, nw, bias, cos, sin_signed)


def _attn_a_kernel(q_ref, k_ref, v_ref, other_rows_hbm, o_ref, *, tq, seq):
    del other_rows_hbm
    t0 = pl.program_id(1) * tq
    geo = []
    for radius, dil in A_GROUPS:
        width = min(tq + 2 * radius, seq)
        start = pl.multiple_of(jnp.clip(t0 - radius, 0, seq - width), 64)
        diff = (start - t0) + lax.broadcasted_iota(jnp.int32, (tq, width), 1) \
            - lax.broadcasted_iota(jnp.int32, (tq, width), 0)
        valid = (jnp.abs(diff) <= radius) & ((diff & (dil - 1)) == 0)
        geo.append((start, width, jnp.where(valid, 0.0, NEG)))
    for hh in range(A_HPG):
        scores = []
        for g, (start, width, mask) in enumerate(geo):
            col = (g * A_HPG + hh) * HEAD
            q = q_ref[:, col:col + HEAD]
            k = k_ref[pl.ds(start, width), col:col + HEAD]
            scores.append(_dot_t(q, k) + mask)
        mx = functools.reduce(jnp.maximum, [jnp.max(s, axis=-1, keepdims=True) for s in scores])
        den = jnp.zeros((tq, 1), F32)
        acc = jnp.zeros((tq, HEAD), F32)
        for g, (start, width, mask) in enumerate(geo):
            col = (g * A_HPG + hh) * HEAD
            p = jnp.exp2(scores[g] - mx)
            den = den + jnp.sum(p, axis=-1, keepdims=True)
            v = v_ref[pl.ds(start, width), col:col + HEAD]
            acc = acc + jnp.dot(p.astype(BF), v, preferred_element_type=F32)
        o_ref[:, hh * HEAD:(hh + 1) * HEAD] = (acc / den).astype(o_ref.dtype)


def attn_a(proj, out, goff, batch, seq, cfg):
    tq = cfg.tq_a
    m = proj.shape[0]
    rb0, sb0 = goff // tq, goff // seq
    nq = seq // tq
    whole = dict(pipeline_mode=pl.Buffered(1))
    return pl.pallas_call(
        functools.partial(_attn_a_kernel, tq=tq, seq=seq),
        grid=(batch, nq),
        in_specs=[
            pl.BlockSpec((tq, MIXW), lambda b, i: (rb0 + b * nq + i, 0)),
            pl.BlockSpec((seq, MIXW), lambda b, i: (sb0 + b, 1), **whole),
            pl.BlockSpec((seq, MIXW), lambda b, i: (sb0 + b, 2), **whole),
            pl.BlockSpec(memory_space=pl.ANY),
        ],
        out_specs=pl.BlockSpec((tq, A_HPG * HEAD), lambda b, i: (rb0 + b * nq + i, 0)),
        out_shape=jax.ShapeDtypeStruct((m, A_HPG * HEAD), BF),
        input_output_aliases={3: 0},
        compiler_params=_params("parallel", "arbitrary"),
        name="attn_a",
    )(proj, proj, proj, out)


def _attn_b_kernel(q_ref, k_ref, v_ref, lam_ref, g_ref, other_rows_hbm, o_ref, s_even, s_odd, *, lam_init, sub):
    del other_rows_hbm
    n = q_ref.shape[0] // sub
    lq = lam_ref[...]
    lam = (jnp.exp(jnp.sum(lq[0:1] * lq[1:2], keepdims=True))
           - jnp.exp(jnp.sum(lq[2:3] * lq[3:4], keepdims=True)) + lam_init)

    def rows_of(i):
        return pl.ds(pl.multiple_of(i * sub, sub), sub)

    def scores(i, dst):
        for c in range(2):
            dst[c] = _dot_t(q_ref[rows_of(i), c * HEAD:(c + 1) * HEAD], k_ref[:, c * HEAD:(c + 1) * HEAD])

    def finish(i, src):
        def unnormalised(c):
            s = src[c]
            p = jnp.exp2(s - jnp.max(s, axis=-1, keepdims=True))
            return p, 1.0 / jnp.sum(p, axis=-1, keepdims=True)

        p0, r0 = unnormalised(0)
        p1, r1 = unnormalised(1)
        a = (p0 * r0 - p1 * (lam * r1)).astype(BF)
        o = jnp.dot(a, v_ref[...], preferred_element_type=F32)
        o_ref[rows_of(i), :] = ((_rms(o) * g_ref[...]) * (1.0 - lam_init)).astype(o_ref.dtype)

    scores(0, s_even)

    def pair(j, carry):
        i = 2 * j
        scores(i + 1, s_odd)
        finish(i, s_even)
        scores(i + 2, s_even)
        finish(i + 1, s_odd)
        return carry

    lax.fori_loop(0, n // 2 - 1, pair, 0)
    scores(n - 1, s_odd)
    finish(n - 2, s_even)
    finish(n - 1, s_odd)


def attn_b(proj, out, lambda_qk, subln, lam_init, goff, batch, seq, cfg):
    tq = min(cfg.tq_b, seq)
    sub = cfg.sub_b
    assert tq % (2 * sub) == 0
    m = proj.shape[0]
    w = 2 * HEAD
    rb0, sb0 = goff // tq, goff // seq
    nq = seq // tq
    c0 = 3 * MIXW // w
    return pl.pallas_call(
        functools.partial(_attn_b_kernel, lam_init=lam_init, sub=sub),
        grid=(batch, B_HEADS, nq),
        in_specs=[
            pl.BlockSpec((tq, w), lambda b, h, i: (rb0 + b * nq + i, c0 + h)),
            pl.BlockSpec((seq, w), lambda b, h, i: (sb0 + b, c0 + B_HEADS + h)),
            pl.BlockSpec((seq, w), lambda b, h, i: (sb0 + b, c0 + 2 * B_HEADS + h)),
            pl.BlockSpec((4, HEAD), lambda b, h, i: (0, 0)),
            pl.BlockSpec((1, w), lambda b, h, i: (0, 0)),
            pl.BlockSpec(memory_space=pl.ANY),
        ],
        out_specs=pl.BlockSpec((tq, w), lambda b, h, i: (rb0 + b * nq + i, h)),
        out_shape=jax.ShapeDtypeStruct((m, MIXW), BF),
        scratch_shapes=[pltpu.VMEM((2, sub, seq), F32), pltpu.VMEM((2, sub, seq), F32)],
        input_output_aliases={5: 0},
        compiler_params=_params("parallel", "parallel", "arbitrary"),
        name="attn_b",
    )(proj, proj, proj, lambda_qk, subln.reshape(1, w), out)


def _natten_bias_index():
    j = np.arange(GRID_W)
    toeplitz = np.clip(j[None, :] - j[:, None], -(NA_COLS - 1), NA_COLS - 1) + NA_COLS - 1
    cs = np.clip(j - NA_COLS // 2, 0, GRID_W - NA_COLS)
    col_ok = (j[None, :] >= cs[:, None]) & (j[None, :] < cs[:, None] + NA_COLS)
    a = np.arange(C_ROWS_PER_BLOCK)
    kb = np.arange(3 * C_ROWS_PER_BLOCK)
    drow = (kb[None, :] - C_ROWS_PER_BLOCK) - a[:, None]
    kc = kb // C_ROWS_PER_BLOCK - 1
    row_ok = np.stack([
        np.broadcast_to((kc >= 0)[None, :], drow.shape),
        (drow >= -(NA_ROWS // 2)) & (drow < NA_ROWS - NA_ROWS // 2),
        np.broadcast_to((kc <= 0)[None, :], drow.shape),
    ])
    ok = row_ok[:, :, None, :, None] & col_ok[None, None, :, None, :]
    rpb_row = np.clip(drow + NA_ROWS - 1, 0, 2 * NA_ROWS - 2)
    return toeplitz.astype(np.int32), rpb_row.astype(np.int32), ok.reshape(3, C_TQ, 3 * C_TQ)


def natten_bias(rpb):
    toeplitz, rpb_row, ok = _natten_bias_index()
    t = rpb[:, :, toeplitz]
    v = t[:, rpb_row]
    v = v.transpose(0, 1, 3, 2, 4).reshape(C_HEADS, C_TQ, 3 * C_TQ)
    return jnp.where(ok[:, None], v[None] * LOG2E, NEG).astype(F32)


def _attn_c_kernel(q_ref, kp_ref, ko_ref, kn_ref, vp_ref, vo_ref, vn_ref, bias_ref, other_rows_hbm, o_ref):
    del other_rows_hbm
    heads = o_ref.shape[1] // HEAD
    for h in range(heads):
        sl = slice(h * HEAD, (h + 1) * HEAD)
        k = jnp.concatenate([kp_ref[:, sl], ko_ref[:, sl], kn_ref[:, sl]], axis=0)
        v = jnp.concatenate([vp_ref[:, sl], vo_ref[:, sl], vn_ref[:, sl]], axis=0)
        s = _dot_t(q_ref[:, sl], k) + bias_ref[h]
        p = jnp.exp2(s - jnp.max(s, axis=-1, keepdims=True))
        o = jnp.dot(p.astype(BF), v, preferred_element_type=F32) / jnp.sum(p, axis=-1, keepdims=True)
        o_ref[:, sl] = o.astype(o_ref.dtype)


def attn_c(proj, out, bias, goff, batch, seq, cfg):
    m = proj.shape[0]
    wb = 4 * HEAD
    nhg = MIXW // wb
    nb = seq // C_TQ
    assert nb >= 3
    rb0 = goff // C_TQ
    cq, ck, cv = 6 * nhg, 7 * nhg, 8 * nhg

    def rows(b, i):
        return rb0 + b * nb + i

    def case(i):
        return jnp.where(i == 0, 0, jnp.where(i == nb - 1, 2, 1))

    def spec(c0, shift):
        return pl.BlockSpec((C_TQ, wb), lambda b, i, g: (rows(b, jnp.clip(i + shift, 0, nb - 1)), c0 + g))

    return pl.pallas_call(
        _attn_c_kernel,
        grid=(batch, nb, nhg),
        in_specs=[
            spec(cq, 0), spec(ck, -1), spec(ck, 0), spec(ck, 1), spec(cv, -1), spec(cv, 0), spec(cv, 1),
            pl.BlockSpec((None, 4, C_TQ, 3 * C_TQ), lambda b, i, g: (case(i), g, 0, 0)),
            pl.BlockSpec(memory_space=pl.ANY),
        ],
        out_specs=pl.BlockSpec((C_TQ, wb), lambda b, i, g: (rows(b, i), g)),
        out_shape=jax.ShapeDtypeStruct((m, MIXW), BF),
        input_output_aliases={8: 0},
        compiler_params=_params("parallel", "parallel", "arbitrary"),
        name="attn_c",
    )(proj, proj, proj, proj, proj, proj, proj, bias, out)


def _merge_kernel(oa_ref, ob_ref, oc_ref, wa_ref, wb_ref, wc_ref, ga_ref, gb_ref, gc_ref, o_ref):
    def br(o, w, g):
        return g[...].astype(F32) * jnp.dot(o[...], w[...].astype(BF), preferred_element_type=F32)

    o_ref[...] = (br(oa_ref, wa_ref, ga_ref) + br(ob_ref, wb_ref, gb_ref) + br(oc_ref, wc_ref, gc_ref)
                  ).astype(o_ref.dtype)


def merge(oa, ob, oc, wa, wb, wc, layer, proj, cfg):
    m = oa.shape[0]
    d = wa.shape[2]
    tm, tn = cfg.tm, cfg.tn
    g0 = QKV_COLS // tn
    gs = d // tn

    def rowblk(width):
        return pl.BlockSpec((tm, width), lambda i, j: (i, 0))

    def wblk(k):
        return pl.BlockSpec((None, k, tn), lambda i, j: (layer, 0, j))

    def gate(n):
        return pl.BlockSpec((tm, tn), lambda i, j: (i, g0 + n * gs + j))

    return pl.pallas_call(
        _merge_kernel,
        grid=(m // tm, d // tn),
        in_specs=[rowblk(oa.shape[1]), rowblk(MIXW), rowblk(MIXW), wblk(wa.shape[1]), wblk(MIXW), wblk(MIXW),
                  gate(0), gate(1), gate(2)],
        out_specs=pl.BlockSpec((tm, tn), lambda i, j: (i, j)),
        out_shape=jax.ShapeDtypeStruct((m, d), BF),
        compiler_params=_params("parallel", "arbitrary"),
        name="merge",
    )(oa, ob, oc, wa, wb, wc, proj, proj, proj)


def _out_proj_kernel(m_ref, w_ref, *refs, bounds, tile0):
    xs, o_ref = refs[:len(bounds)], refs[-1]
    acc = jnp.dot(m_ref[...], w_ref[...].astype(BF), preferred_element_type=F32)
    if len(bounds) == 1:
        o_ref[...] = xs[0][...] + acc
        return
    i = pl.program_id(0) + tile0
    for x_ref, (b0, b1) in zip(xs, bounds):
        @pl.when((i >= b0) & (i < b1))
        def _():
            o_ref[...] = x_ref[...] + acc


def out_proj(merged, w_out, layer, x_parts, cfg, out_parts=None):
    m, d = merged.shape
    tm, tn = cfg.tm, cfg.tn
    bounds = _row_parts(x_parts, tm)
    last = d // tn - 1
    outs = []
    for rows, off in (out_parts or [(m, 0)]):
        t0 = off // tm

        def residual(b0, b1, t0=t0):
            return pl.BlockSpec((tm, tn), lambda i, j: (_part_row(t0 + i, b0, b1),
                                                        jnp.where(t0 + i < b0, 0, jnp.where(t0 + i >= b1, last, j))))

        outs.append(pl.pallas_call(
            functools.partial(_out_proj_kernel, bounds=bounds, tile0=t0),
            grid=(rows // tm, d // tn),
            in_specs=[
                pl.BlockSpec((tm, d), lambda i, j, t0=t0: (t0 + i, 0)),
                pl.BlockSpec((None, d, tn), lambda i, j: (layer, 0, j)),
            ] + [residual(b0, b1) for b0, b1 in bounds],
            out_specs=pl.BlockSpec((tm, tn), lambda i, j: (i, j)),
            out_shape=jax.ShapeDtypeStruct((rows, d), F32),
            compiler_params=_params("parallel", "arbitrary"),
            name="out_proj",
        )(merged, w_out, *[x for x, _ in x_parts]))
    return outs if out_parts else outs[0]


def _router_kernel(*refs, bounds, n_experts):
    xs, g_ref, w_ref, o_ref = refs[:len(bounds)], refs[-3], refs[-2], refs[-1]

    def split(v):
        hi = v.astype(BF)
        return hi, (v - hi.astype(F32)).astype(BF)

    def affinities(x_ref):
        h_hi, h_lo = split(_rms(x_ref[...]) * g_ref[...])
        w_hi, w_lo = split(w_ref[...])
        mm = functools.partial(jnp.dot, preferred_element_type=F32)
        logits = mm(h_hi, w_hi) + (mm(h_hi, w_lo) + mm(h_lo, w_hi))
        lane = lax.broadcasted_iota(jnp.int32, logits.shape, 1)
        logits = jnp.where(lane < n_experts, logits, NEG)
        p = jnp.exp(logits - jnp.max(logits, axis=-1, keepdims=True))
        o_ref[...] = p / jnp.sum(p, axis=-1, keepdims=True)

    if len(bounds) == 1:
        affinities(xs[0])
        return
    i = pl.program_id(0)
    for x_ref, (b0, b1) in zip(xs, bounds):
        @pl.when((i >= b0) & (i < b1))
        def _():
            affinities(x_ref)


def router(x_parts, g, w_router, m, cfg):
    d = x_parts[0][0].shape[1]
    tm = min(512, cfg.tm)
    e = w_router.shape[1]
    wpad = jnp.zeros((d, HEAD), F32).at[:, :e].set(w_router)
    bounds = _row_parts(x_parts, tm)
    return pl.pallas_call(
        functools.partial(_router_kernel, bounds=bounds, n_experts=e),
        grid=(m // tm,),
        in_specs=[pl.BlockSpec((tm, d), lambda i, b0=b0, b1=b1: (_part_row(i, b0, b1), 0)) for b0, b1 in bounds]
        + [pl.BlockSpec((1, d), lambda i: (0, 0)), pl.BlockSpec((d, HEAD), lambda i: (0, 0))],
        out_specs=pl.BlockSpec((tm, HEAD), lambda i: (i, 0)),
        out_shape=jax.ShapeDtypeStruct((m, HEAD), F32),
        compiler_params=_params("parallel"),
        name="router",
    )(*[x for x, _ in x_parts], g.reshape(1, d), wpad)


def _prefix_incl(x01, upper):
    within = jnp.dot(x01.astype(BF), upper, preferred_element_type=F32)
    totals = jnp.broadcast_to(within[:, SEL_CHUNK - 1:SEL_CHUNK], within.shape)
    chunk = lax.broadcasted_iota(jnp.int32, within.shape, 0)
    before = jnp.zeros_like(within)
    for j in range(within.shape[0] - 1):
        before = before + jnp.where(chunk > j, totals[j:j + 1, :], 0.0)
    return within + before


def _topc_kernel(a_ref, tab_ref, idx_ref, gate_ref, *, cap, n_experts):
    a = a_ref[...]
    nch = a.shape[0]
    bits = pltpu.bitcast(a, jnp.int32)

    def count(mask):
        return jnp.sum(jnp.where(mask, 1.0, 0.0), keepdims=True)

    def keeps(cand):
        return jnp.where(count(bits >= cand) >= cap, 1, 0)

    def two_bits(i, t):
        shift = 28 - 2 * i
        digit = keeps(t | jnp.left_shift(jnp.int32(1), shift))
        for d in (2, 3):
            digit = digit + keeps(t | jnp.left_shift(jnp.int32(d), shift))
        return t | jnp.left_shift(digit, shift)

    top = jnp.left_shift(keeps(jnp.full((1, 1), 1 << 30, jnp.int32)), 30)
    thr = lax.fori_loop(0, 15, two_bits, top)
    r_i = lax.broadcasted_iota(jnp.int32, (SEL_CHUNK, SEL_CHUNK), 0)
    c_i = lax.broadcasted_iota(jnp.int32, (SEL_CHUNK, SEL_CHUNK), 1)
    upper = jnp.where(r_i <= c_i, 1.0, 0.0).astype(BF)

    gt = bits > thr
    eq = bits == thr
    room = cap - count(gt)
    eq_f = jnp.where(eq, 1.0, 0.0)
    eq_before = _prefix_incl(eq_f, upper) - eq_f
    sel = gt | (eq & (eq_before < room))
    rank = jnp.where(sel, _prefix_incl(jnp.where(sel, 1.0, 0.0), upper), 0.0)

    want = (lax.broadcasted_iota(jnp.int32, (cap, SEL_CHUNK), 0) + 1).astype(F32)
    acc = jnp.zeros((cap, HEAD), F32)
    for j in range(nch):
        hit = jnp.where(rank[j:j + 1, :] == want, 1.0, 0.0).astype(BF)
        acc = acc + jnp.dot(hit, tab_ref[j * SEL_CHUNK:(j + 1) * SEL_CHUNK, :], preferred_element_type=F32)
    lane = lax.broadcasted_iota(jnp.int32, acc.shape, 1)

    def pick(l):
        return jnp.sum(jnp.where(lane == l, acc, 0.0), axis=-1, keepdims=True)

    e = pl.program_id(0)
    gate_ref[...] = (pick(e) + pick(e + n_experts)) + pick(e + 2 * n_experts)
    idx_ref[...] = (pick(3 * n_experts + 1) * SEL_CHUNK + pick(3 * n_experts)).astype(jnp.int32)


def topc_table(aff):
    n, e = aff.shape
    assert 3 * e + 2 <= HEAD and n // SEL_CHUNK <= 256
    hi = aff.astype(BF)
    rest = aff - hi.astype(F32)
    mid = rest.astype(BF)
    lo = (rest - mid.astype(F32)).astype(BF)
    tok = jnp.arange(n, dtype=jnp.int32)
    pos = (tok % SEL_CHUNK).astype(BF)[:, None]
    chunk = (tok // SEL_CHUNK).astype(BF)[:, None]
    return jnp.concatenate([hi, mid, lo, pos, chunk, jnp.zeros((n, HEAD - 3 * e - 2), BF)], axis=1)


def topc(aff, cap):
    n, e = aff.shape
    nch = n // SEL_CHUNK
    return pl.pallas_call(
        functools.partial(_topc_kernel, cap=cap, n_experts=e),
        grid=(e,),
        in_specs=[pl.BlockSpec((None, nch, SEL_CHUNK), lambda i: (i, 0, 0)),
                  pl.BlockSpec((n, HEAD), lambda i: (0, 0))],
        out_specs=[pl.BlockSpec((None, cap, 1), lambda i: (i, 0, 0)), pl.BlockSpec((None, cap, 1), lambda i: (i, 0, 0))],
        out_shape=[jax.ShapeDtypeStruct((e, cap, 1), jnp.int32), jax.ShapeDtypeStruct((e, cap, 1), F32)],
        compiler_params=_params("parallel"),
        name="topc",
    )(aff.T.reshape(e, nch, SEL_CHUNK), topc_table(aff))


def _row_copy(src_hbm, dst, sem, tok, r):
    return pltpu.make_async_copy(src_hbm.at[pl.ds(tok, 1), :], dst.at[pl.ds(r, 1), :], sem)


ROW_LOOP_UNROLL = 8


def _gather_norm_kernel(idx_ref, *refs, tr, offs, tiles_per_list):
    srcs, (g_ref, o_ref, buf, sem) = refs[:len(offs)], refs[len(offs):]
    i = pl.program_id(0)
    slot = i % 2

    def rows(tile, s, go):
        def loop(src, off):
            def body(r, c):
                cp = _row_copy(src, buf.at[s], sem.at[s], idx_ref[tile * tr + r] - off, r)
                cp.start() if go else cp.wait()
                return c
            lax.fori_loop(0, tr, body, 0, unroll=ROW_LOOP_UNROLL)

        if len(srcs) == 1:
            loop(srcs[0], offs[0])
            return
        group = (tile // tiles_per_list) % len(srcs)
        for p, (src, off) in enumerate(zip(srcs, offs)):
            @pl.when(group == p)
            def _():
                loop(src, off)

    @pl.when(i == 0)
    def _():
        rows(0, 0, True)

    @pl.when(i + 1 < pl.num_programs(0))
    def _():
        rows(i + 1, 1 - slot, True)

    rows(i, slot, False)
    o_ref[...] = (_rms(buf[slot]) * g_ref[...]).astype(o_ref.dtype)


def gather_norm(x_parts, g, idx, cap, cfg):
    d = x_parts[0][0].shape[1]
    r = idx.shape[0]
    tr = cfg.tr
    assert cap % tr == 0 and len(x_parts) in (1, len(cfg.groups))
    return pl.pallas_call(
        functools.partial(_gather_norm_kernel, tr=tr, offs=tuple(o for _, o in x_parts), tiles_per_list=cap // tr),
        grid_spec=pltpu.PrefetchScalarGridSpec(
            num_scalar_prefetch=1,
            grid=(r // tr,),
            in_specs=[pl.BlockSpec(memory_space=pl.ANY)] * len(x_parts) + [pl.BlockSpec((1, d), lambda i, idx: (0, 0))],
            out_specs=pl.BlockSpec((tr, d), lambda i, idx: (i, 0)),
            scratch_shapes=[pltpu.VMEM((2, tr, d), F32), pltpu.SemaphoreType.DMA((2,))],
        ),
        out_shape=jax.ShapeDtypeStruct((r, d), BF),
        compiler_params=_params("arbitrary"),
        name="gather_norm",
    )(idx, *[x for x, _ in x_parts], g.reshape(1, d))


def _ffn_up_kernel(x_ref, wg_ref, wu_ref, o_ref):
    x = x_ref[...]
    gate = jnp.dot(x, wg_ref[...].astype(BF), preferred_element_type=F32)
    up = jnp.dot(x, wu_ref[...].astype(BF), preferred_element_type=F32)
    o_ref[...] = (jax.nn.silu(gate) * up).astype(o_ref.dtype)


def ffn_up(xe, w_gate, w_up, layer, cfg):
    _, e, d, f = w_gate.shape
    rows = xe.shape[0] // e
    tf = cfg.tf
    return pl.pallas_call(
        _ffn_up_kernel,
        grid=(e, f // tf),
        in_specs=[
            pl.BlockSpec((rows, d), lambda i, j: (i, 0), pipeline_mode=pl.Buffered(1)),
            pl.BlockSpec((None, None, d, tf), lambda i, j: (layer, i, 0, j)),
            pl.BlockSpec((None, None, d, tf), lambda i, j: (layer, i, 0, j)),
        ],
        out_specs=pl.BlockSpec((rows, tf), lambda i, j: (i, j)),
        out_shape=jax.ShapeDtypeStruct((xe.shape[0], f), BF),
        compiler_params=_params("parallel", "arbitrary"),
        name="ffn_up",
    )(xe, w_gate, w_up)


def _ffn_down_kernel(h_ref, w_ref, g_ref, o_ref):
    o_ref[...] = jnp.dot(h_ref[...], w_ref[...].astype(BF), preferred_element_type=F32) * g_ref[...]


def ffn_down(hid, w_down, layer, gate, cfg):
    _, e, f, d = w_down.shape
    rows = hid.shape[0] // e
    tn = cfg.tn
    return pl.pallas_call(
        _ffn_down_kernel,
        grid=(e, d // tn),
        in_specs=[
            pl.BlockSpec((rows, f), lambda i, j: (i, 0)),
            pl.BlockSpec((None, None, f, tn), lambda i, j: (layer, i, 0, j)),
            pl.BlockSpec((rows, 1), lambda i, j: (i, 0)),
        ],
        out_specs=pl.BlockSpec((rows, tn), lambda i, j: (i, j)),
        out_shape=jax.ShapeDtypeStruct((hid.shape[0], d), F32),
        compiler_params=_params("parallel", "arbitrary"),
        name="ffn_down",
    )(hid, w_down, gate)


def _scatter_add_kernel(idx_ref, y_ref, x_hbm, o_hbm, buf, sem, *, tr, off, tile_of, tiles_per_expert):
    del x_hbm
    i = pl.program_id(0)
    tile = tile_of(i)
    slot = i % 2
    pos = i % tiles_per_expert
    first = pos == 0
    last = pos == tiles_per_expert - 1

    def fetch(t, s, r):
        tok = idx_ref[t * tr + r] - off
        return pltpu.make_async_copy(o_hbm.at[pl.ds(tok, 1), :], buf.at[s, pl.ds(r, 1), :], sem.at[s])

    def store(t, s, r):
        tok = idx_ref[t * tr + r] - off
        return pltpu.make_async_copy(buf.at[s, pl.ds(r, 1), :], o_hbm.at[pl.ds(tok, 1), :], sem.at[2 + s])

    def each(fn):
        def body(r, c):
            fn(r)
            return c
        lax.fori_loop(0, tr, body, 0, unroll=ROW_LOOP_UNROLL)

    @pl.when(first)
    def _():
        each(lambda r: fetch(tile, slot, r).start())

    @pl.when(jnp.logical_not(first))
    def _():
        each(lambda r: store(tile - 1, 1 - slot, r).wait())

    @pl.when(jnp.logical_not(last))
    def _():
        each(lambda r: fetch(tile + 1, 1 - slot, r).start())

    each(lambda r: fetch(tile, slot, r).wait())
    buf[slot] = buf[slot] + y_ref[...]
    each(lambda r: store(tile, slot, r).start())

    @pl.when(last)
    def _():
        each(lambda r: store(tile, slot, r).wait())


def scatter_add(x_parts, ye, idx, cap, cfg):
    d = ye.shape[1]
    r = idx.shape[0]
    tr = cfg.tr
    n_parts = len(x_parts)
    per_expert = r // cfg.n_experts // tr
    mine = per_expert // n_parts
    assert cap % tr == 0 and n_parts in (1, len(cfg.groups)) and mine * n_parts == per_expert
    outs = []
    for p, (x, off) in enumerate(x_parts):
        def tile_of(i, p=p):
            return (i // mine) * per_expert + p * mine + i % mine

        outs.append(pl.pallas_call(
            functools.partial(_scatter_add_kernel, tr=tr, off=off, tile_of=tile_of, tiles_per_expert=mine),
            grid_spec=pltpu.PrefetchScalarGridSpec(
                num_scalar_prefetch=1,
                grid=(cfg.n_experts * mine,),
                in_specs=[pl.BlockSpec((tr, d), lambda i, idx, tile_of=tile_of: (tile_of(i), 0)),
                          pl.BlockSpec(memory_space=pl.ANY)],
                out_specs=pl.BlockSpec(memory_space=pl.ANY),
                scratch_shapes=[pltpu.VMEM((2, tr, d), F32), pltpu.SemaphoreType.DMA((4,))],
            ),
            out_shape=jax.ShapeDtypeStruct(x.shape, F32),
            input_output_aliases={2: 0},
            compiler_params=_params("arbitrary"),
            name="scatter_add",
        )(idx, ye, x))
    return outs


def _rope_tables(cfg):
    inv = 1.0 / (ROPE_THETA ** (jnp.arange(0, HEAD, 2, dtype=F32) / HEAD))
    cos, sin = [], []
    sign = jnp.where(jnp.arange(HEAD) < HEAD // 2, -1.0, 1.0).astype(F32)
    for batch, seq in cfg.groups:
        ang = jnp.arange(seq, dtype=F32)[:, None] * inv[None, :]
        ang = jnp.concatenate([ang, ang], axis=-1)
        cos.append(jnp.tile(jnp.cos(ang), (batch, 1)))
        sin.append(jnp.tile(jnp.sin(ang) * sign, (batch, 1)))
    return jnp.concatenate(cos, axis=0), jnp.concatenate(sin, axis=0)


def _group_offsets(cfg):
    offs, o = [], 0
    for batch, seq in cfg.groups:
        offs.append(o)
        o += batch * seq
    return offs, o


def _mixer(x_parts, l, cos, sin, p, cfg, out_parts):
    d = cfg.d_model
    offs, m = _group_offsets(cfg)
    ones = jnp.ones((MIXW,), F32)
    heads = MIXW // HEAD
    qk = p["qk_norm"][l].astype(F32)
    nw = jnp.concatenate([
        jnp.tile(qk[0] * Q_SCALE, heads), jnp.tile(qk[1], heads), ones,
        jnp.tile(qk[2] * Q_SCALE, heads), jnp.tile(qk[3], heads), ones,
        jnp.tile(qk[4] * Q_SCALE, heads), jnp.tile(qk[5], heads), ones,
        jnp.ones((3 * d,), F32)]).reshape(1, -1)
    bias = jnp.concatenate([jnp.zeros((QKV_COLS,), F32), p["b_gate"][l].astype(F32)]).reshape(1, -1)

    h = rmsnorm_bf16(x_parts, p["g_mix"][l], m, min(512, cfg.tm))
    proj = in_proj(h, p["w_in"], l, nw, bias, cos, sin, cfg)

    lam_init = 0.8 - 0.6 * math.exp(-0.3 * l)
    cbias = natten_bias(p["rpb"][l].astype(F32))
    oa = jnp.zeros((m, A_HPG * HEAD), BF)
    ob = jnp.zeros((m, MIXW), BF)
    oc = jnp.zeros((m, MIXW), BF)
    for (batch, seq), goff in zip(cfg.groups, offs):
        oa = attn_a(proj, oa, goff, batch, seq, cfg)
        ob = attn_b(proj, ob, p["lambda_qk"][l].astype(F32), p["subln"][l].astype(F32), lam_init, goff, batch, seq, cfg)
        oc = attn_c(proj, oc, cbias, goff, batch, seq, cfg)
    merged = merge(oa, ob, oc, p["w_br_a"], p["w_br_b"], p["w_br_c"], l, proj, cfg)
    new = out_proj(merged, p["w_out"], l, x_parts, cfg, out_parts)
    return [(x, off) for x, (_, off) in zip(new, out_parts)] if out_parts else [(new, 0)]


def _moe(x_parts, l, p, cfg):
    e = cfg.n_experts
    offs, m = _group_offsets(cfg)
    aff = router(x_parts, p["g_ffn"][l], p["w_router"][l], m, cfg)[:, :e]
    idx, gate = [], []
    caps = {CAP_FACTOR * batch * seq // e for batch, seq in cfg.groups}
    assert len(caps) == 1
    cap = caps.pop()
    for (batch, seq), goff in zip(cfg.groups, offs):
        n = batch * seq
        i_g, g_g = topc(aff[goff:goff + n], cap)
        idx.append(i_g.reshape(e, cap) + goff)
        gate.append(g_g.reshape(e, cap))
    idx = jnp.concatenate(idx, axis=1).reshape(-1)
    gate = jnp.concatenate(gate, axis=1).reshape(-1, 1)
    xe = gather_norm(x_parts, p["g_ffn"][l], idx, cap, cfg)
    hid = ffn_up(xe, p["w_e_gate"], p["w_e_up"], l, cfg)
    ye = ffn_down(hid, p["w_e_down"], l, gate, cfg)
    return [(x, off) for x, (_, off) in zip(scatter_add(x_parts, ye, idx, cap, cfg), x_parts)]


def trunk(xs, p, cfg):
    d = cfg.d_model
    offs, _ = _group_offsets(cfg)
    x_parts = [(a.reshape(-1, d), o) for a, o in zip(xs, offs)]
    by_group = [(b * s, o) for (b, s), o in zip(cfg.groups, offs)]
    cos, sin = _rope_tables(cfg)
    for l in range(cfg.depth):
        out_parts = by_group if l == cfg.depth - 1 else None
        x_parts = _moe(_mixer(x_parts, l, cos, sin, p, cfg, out_parts), l, p, cfg)
    return tuple(x.reshape(b, s, d) for (x, _), (b, s) in zip(x_parts, cfg.groups))


def kernel(x_prompt, x_sample, g_mix, w_in, b_gate, qk_norm, lambda_qk, subln, rpb, w_br_a, w_br_b, w_br_c,
           w_out, g_ffn, w_router, w_e_gate, w_e_up, w_e_down):
    cfg = Cfg(d_model=x_prompt.shape[-1],
              groups=(x_prompt.shape[:2], x_sample.shape[:2]),
              n_experts=w_router.shape[-1], d_ff=w_e_gate.shape[-1], depth=g_mix.shape[0])
    p = dict(g_mix=g_mix, w_in=w_in, b_gate=b_gate, qk_norm=qk_norm, lambda_qk=lambda_qk, subln=subln, rpb=rpb,
             w_br_a=w_br_a, w_br_b=w_br_b, w_br_c=w_br_c, w_out=w_out, g_ffn=g_ffn, w_router=w_router,
             w_e_gate=w_e_gate, w_e_up=w_e_up, w_e_down=w_e_down)
    return trunk((x_prompt, x_sample), p, cfg)
```

```python
import functools
import math
from typing import NamedTuple

import numpy as np
import jax
import jax.numpy as jnp
from jax import lax
from jax.experimental import pallas as pl
from jax.experimental.pallas import tpu as pltpu

HEAD = 128
MIXW = 1536
QKV_COLS = 9 * MIXW
A_GROUPS = ((64, 1), (256, 4), (1024, 16))
A_HPG = 4
B_HEADS = 6
C_HEADS = 12
GRID_W = 64
NA_ROWS = 8
NA_COLS = 16
C_ROWS_PER_BLOCK = 4
C_TQ = C_ROWS_PER_BLOCK * GRID_W
EPS = 1e-6
NEG = -1e30
LOG2E = math.log2(math.e)
Q_SCALE = HEAD ** -0.5 * LOG2E
ROPE_THETA = 10000.0
CAP_FACTOR = 2
SEL_CHUNK = 256
VMEM_LIMIT = 56 * 1024 * 1024

BF = jnp.bfloat16
F32 = jnp.float32


class Cfg(NamedTuple):
    d_model: int
    groups: tuple
    n_experts: int
    d_ff: int
    depth: int
    tm: int = 1024
    tn: int = 512
    tq_a: int = 128
    tq_b: int = 1024
    sub_b: int = 256
    tr: int = 256
    tf: int = 256


def _params(*sem):
    return pltpu.CompilerParams(dimension_semantics=sem, vmem_limit_bytes=VMEM_LIMIT)


def _rms(x):
    return x * lax.rsqrt(jnp.mean(x * x, axis=-1, keepdims=True) + EPS)


def _dot_t(a, b):
    return lax.dot_general(a, b, (((1,), (1,)), ((), ())), preferred_element_type=F32)


def _row_parts(parts, tm):
    return [(off // tm, (off + x.shape[0]) // tm) for x, off in parts]


def _part_row(i, b0, b1):
    return jnp.clip(i - b0, 0, b1 - b0 - 1)


def _rmsnorm_kernel(*refs, bounds):
    xs, g_ref, o_ref = refs[:len(bounds)], refs[-2], refs[-1]
    if len(bounds) == 1:
        o_ref[...] = (_rms(xs[0][...]) * g_ref[...]).astype(o_ref.dtype)
        return
    i = pl.program_id(0)
    for x_ref, (b0, b1) in zip(xs, bounds):
        @pl.when((i >= b0) & (i < b1))
        def _():
            o_ref[...] = (_rms(x_ref[...]) * g_ref[...]).astype(o_ref.dtype)


def rmsnorm_bf16(parts, g, m, tm):
    d = parts[0][0].shape[1]
    bounds = _row_parts(parts, tm)
    return pl.pallas_call(
        functools.partial(_rmsnorm_kernel, bounds=bounds),
        grid=(m // tm,),
        in_specs=[pl.BlockSpec((tm, d), lambda i, b0=b0, b1=b1: (_part_row(i, b0, b1), 0)) for b0, b1 in bounds]
        + [pl.BlockSpec((1, d), lambda i: (0, 0))],
        out_specs=pl.BlockSpec((tm, d), lambda i: (i, 0)),
        out_shape=jax.ShapeDtypeStruct((m, d), BF),
        compiler_params=_params("parallel"),
        name="rmsnorm",
    )(*[x for x, _ in parts], g.reshape(1, d))


def _in_proj_kernel(h_ref, w_ref, nw_ref, b_ref, cos_ref, sin_ref, o_ref, *, tn):
    seg = (pl.program_id(1) * tn) // MIXW
    acc = jnp.dot(h_ref[...], w_ref[...].astype(BF), preferred_element_type=F32)
    heads = tn // HEAD

    def normed(h):
        sl = slice(h * HEAD, (h + 1) * HEAD)
        return _rms(acc[:, sl]) * nw_ref[:, sl]

    @pl.when((seg == 0) | (seg == 1) | (seg == 3) | (seg == 4))
    def _():
        for h in range(heads):
            y = normed(h)
            y = y * cos_ref[...] + pltpu.roll(y, HEAD // 2, 1) * sin_ref[...]
            o_ref[:, h * HEAD:(h + 1) * HEAD] = y.astype(o_ref.dtype)

    @pl.when((seg == 6) | (seg == 7))
    def _():
        for h in range(heads):
            o_ref[:, h * HEAD:(h + 1) * HEAD] = normed(h).astype(o_ref.dtype)

    @pl.when((seg == 2) | (seg == 5) | (seg == 8))
    def _():
        o_ref[...] = acc.astype(o_ref.dtype)

    @pl.when(seg >= 9)
    def _():
        o_ref[...] = (0.5 * jnp.tanh(0.5 * (acc + b_ref[...])) + 0.5).astype(o_ref.dtype)


def in_proj(h, w_in, layer, nw, bias, cos, sin_signed, cfg):
    m, d = h.shape
    ncol = w_in.shape[2]
    tm, tn = cfg.tm, cfg.tn
    return pl.pallas_call(
        functools.partial(_in_proj_kernel, tn=tn),
        grid=(m // tm, ncol // tn),
        in_specs=[
            pl.BlockSpec((tm, d), lambda i, j: (i, 0)),
            pl.BlockSpec((None, d, tn), lambda i, j: (layer, 0, j)),
            pl.BlockSpec((1, tn), lambda i, j: (0, j)),
            pl.BlockSpec((1, tn), lambda i, j: (0, j)),
            pl.BlockSpec((tm, HEAD), lambda i, j: (i, 0)),
            pl.BlockSpec((tm, HEAD), lambda i, j: (i, 0)),
        ],
        out_specs=pl.BlockSpec((tm, tn), lambda i, j: (i, j)),
        out_shape=jax.ShapeDtypeStruct((m, ncol), BF),
        compiler_params=_params("parallel", "arbitrary"),
        name="in_proj",
    )(h, w_in, nw, bias, cos, sin_signed)


def _attn_a_kernel(q_ref, k_ref, v_ref, o_ref, *, tq, seq):
    t0 = pl.program_id(1) * tq
    geo = []
    for radius, dil in A_GROUPS:
        width = min(tq + 2 * radius, seq)
        start = pl.multiple_of(jnp.clip(t0 - radius, 0, seq - width), 64)
        diff = (start - t0) + lax.broadcasted_iota(jnp.int32, (tq, width), 1) \
            - lax.broadcasted_iota(jnp.int32, (tq, width), 0)
        valid = (jnp.abs(diff) <= radius) & ((diff & (dil - 1)) == 0)
        geo.append((start, width, jnp.where(valid, 0.0, NEG)))
    for hh in range(A_HPG):
        scores = []
        for g, (start, width, mask) in enumerate(geo):
            col = (g * A_HPG + hh) * HEAD
            q = q_ref[:, col:col + HEAD]
            k = k_ref[pl.ds(start, width), col:col + HEAD]
            scores.append(_dot_t(q, k) + mask)
        mx = functools.reduce(jnp.maximum, [jnp.max(s, axis=-1, keepdims=True) for s in scores])
        den = jnp.zeros((tq, 1), F32)
        acc = jnp.zeros((tq, HEAD), F32)
        for g, (start, width, mask) in enumerate(geo):
            col = (g * A_HPG + hh) * HEAD
            p = jnp.exp2(scores[g] - mx)
            den = den + jnp.sum(p, axis=-1, keepdims=True)
            v = v_ref[pl.ds(start, width), col:col + HEAD]
            acc = acc + jnp.dot(p.astype(BF), v, preferred_element_type=F32)
        o_ref[:, hh * HEAD:(hh + 1) * HEAD] = (acc / den).astype(o_ref.dtype)


def attn_a(proj, goff, batch, seq, cfg):
    tq = cfg.tq_a
    rb0, sb0 = goff // tq, goff // seq
    nq = seq // tq
    whole = dict(pipeline_mode=pl.Buffered(1))
    return pl.pallas_call(
        functools.partial(_attn_a_kernel, tq=tq, seq=seq),
        grid=(batch, nq),
        in_specs=[
            pl.BlockSpec((tq, MIXW), lambda b, i: (rb0 + b * nq + i, 0)),
            pl.BlockSpec((seq, MIXW), lambda b, i: (sb0 + b, 1), **whole),
            pl.BlockSpec((seq, MIXW), lambda b, i: (sb0 + b, 2), **whole),
        ],
        out_specs=pl.BlockSpec((tq, A_HPG * HEAD), lambda b, i: (b * nq + i, 0)),
        out_shape=jax.ShapeDtypeStruct((batch * seq, A_HPG * HEAD), BF),
        compiler_params=_params("parallel", "arbitrary"),
        name="attn_a",
    )(proj, proj, proj)


def _attn_b_kernel(q_ref, k_ref, v_ref, lam_ref, g_ref, o_ref, s_even, s_odd, *, lam_init, sub):
    n = q_ref.shape[0] // sub
    lq = lam_ref[...]
    lam = (jnp.exp(jnp.sum(lq[0:1] * lq[1:2], keepdims=True))
           - jnp.exp(jnp.sum(lq[2:3] * lq[3:4], keepdims=True)) + lam_init)

    def rows_of(i):
        return pl.ds(pl.multiple_of(i * sub, sub), sub)

    def scores(i, dst):
        for c in range(2):
            dst[c] = _dot_t(q_ref[rows_of(i), c * HEAD:(c + 1) * HEAD], k_ref[:, c * HEAD:(c + 1) * HEAD])

    def finish(i, src):
        def unnormalised(c):
            s = src[c]
            p = jnp.exp2(s - jnp.max(s, axis=-1, keepdims=True))
            return p, 1.0 / jnp.sum(p, axis=-1, keepdims=True)

        p0, r0 = unnormalised(0)
        p1, r1 = unnormalised(1)
        a = (p0 * r0 - p1 * (lam * r1)).astype(BF)
        o = jnp.dot(a, v_ref[...], preferred_element_type=F32)
        o_ref[rows_of(i), :] = ((_rms(o) * g_ref[...]) * (1.0 - lam_init)).astype(o_ref.dtype)

    scores(0, s_even)

    def pair(j, carry):
        i = 2 * j
        scores(i + 1, s_odd)
        finish(i, s_even)
        scores(i + 2, s_even)
        finish(i + 1, s_odd)
        return carry

    lax.fori_loop(0, n // 2 - 1, pair, 0)
    scores(n - 1, s_odd)
    finish(n - 2, s_even)
    finish(n - 1, s_odd)


def attn_b(proj, lambda_qk, subln, lam_init, goff, batch, seq, cfg):
    tq = min(cfg.tq_b, seq)
    sub = cfg.sub_b
    assert tq % (2 * sub) == 0
    w = 2 * HEAD
    rb0, sb0 = goff // tq, goff // seq
    nq = seq // tq
    c0 = 3 * MIXW // w
    return pl.pallas_call(
        functools.partial(_attn_b_kernel, lam_init=lam_init, sub=sub),
        grid=(batch, B_HEADS, nq),
        in_specs=[
            pl.BlockSpec((tq, w), lambda b, h, i: (rb0 + b * nq + i, c0 + h)),
            pl.BlockSpec((seq, w), lambda b, h, i: (sb0 + b, c0 + B_HEADS + h)),
            pl.BlockSpec((seq, w), lambda b, h, i: (sb0 + b, c0 + 2 * B_HEADS + h)),
            pl.BlockSpec((4, HEAD), lambda b, h, i: (0, 0)),
            pl.BlockSpec((1, w), lambda b, h, i: (0, 0)),
        ],
        out_specs=pl.BlockSpec((tq, w), lambda b, h, i: (b * nq + i, h)),
        out_shape=jax.ShapeDtypeStruct((batch * seq, MIXW), BF),
        scratch_shapes=[pltpu.VMEM((2, sub, seq), F32), pltpu.VMEM((2, sub, seq), F32)],
        compiler_params=_params("parallel", "parallel", "arbitrary"),
        name="attn_b",
    )(proj, proj, proj, lambda_qk, subln.reshape(1, w))


def _natten_bias_index():
    j = np.arange(GRID_W)
    toeplitz = np.clip(j[None, :] - j[:, None], -(NA_COLS - 1), NA_COLS - 1) + NA_COLS - 1
    cs = np.clip(j - NA_COLS // 2, 0, GRID_W - NA_COLS)
    col_ok = (j[None, :] >= cs[:, None]) & (j[None, :] < cs[:, None] + NA_COLS)
    a = np.arange(C_ROWS_PER_BLOCK)
    kb = np.arange(3 * C_ROWS_PER_BLOCK)
    drow = (kb[None, :] - C_ROWS_PER_BLOCK) - a[:, None]
    kc = kb // C_ROWS_PER_BLOCK - 1
    row_ok = np.stack([
        np.broadcast_to((kc >= 0)[None, :], drow.shape),
        (drow >= -(NA_ROWS // 2)) & (drow < NA_ROWS - NA_ROWS // 2),
        np.broadcast_to((kc <= 0)[None, :], drow.shape),
    ])
    ok = row_ok[:, :, None, :, None] & col_ok[None, None, :, None, :]
    rpb_row = np.clip(drow + NA_ROWS - 1, 0, 2 * NA_ROWS - 2)
    return toeplitz.astype(np.int32), rpb_row.astype(np.int32), ok.reshape(3, C_TQ, 3 * C_TQ)


def natten_bias(rpb):
    toeplitz, rpb_row, ok = _natten_bias_index()
    t = rpb[:, :, toeplitz]
    v = t[:, rpb_row]
    v = v.transpose(0, 1, 3, 2, 4).reshape(C_HEADS, C_TQ, 3 * C_TQ)
    return jnp.where(ok[:, None], v[None] * LOG2E, NEG).astype(F32)


def _attn_c_kernel(q_ref, kp_ref, ko_ref, kn_ref, vp_ref, vo_ref, vn_ref, bias_ref, o_ref):
    heads = o_ref.shape[1] // HEAD
    for h in range(heads):
        sl = slice(h * HEAD, (h + 1) * HEAD)
        k = jnp.concatenate([kp_ref[:, sl], ko_ref[:, sl], kn_ref[:, sl]], axis=0)
        v = jnp.concatenate([vp_ref[:, sl], vo_ref[:, sl], vn_ref[:, sl]], axis=0)
        s = _dot_t(q_ref[:, sl], k) + bias_ref[h]
        p = jnp.exp2(s - jnp.max(s, axis=-1, keepdims=True))
        o = jnp.dot(p.astype(BF), v, preferred_element_type=F32) / jnp.sum(p, axis=-1, keepdims=True)
        o_ref[:, sl] = o.astype(o_ref.dtype)


def attn_c(proj, bias, goff, batch, seq, cfg):
    wb = 4 * HEAD
    nhg = MIXW // wb
    nb = seq // C_TQ
    assert nb >= 3
    rb0 = goff // C_TQ
    cq, ck, cv = 6 * nhg, 7 * nhg, 8 * nhg

    def rows(b, i):
        return rb0 + b * nb + i

    def case(i):
        return jnp.where(i == 0, 0, jnp.where(i == nb - 1, 2, 1))

    def spec(c0, shift):
        return pl.BlockSpec((C_TQ, wb), lambda b, i, g: (rows(b, jnp.clip(i + shift, 0, nb - 1)), c0 + g))

    return pl.pallas_call(
        _attn_c_kernel,
        grid=(batch, nb, nhg),
        in_specs=[
            spec(cq, 0), spec(ck, -1), spec(ck, 0), spec(ck, 1), spec(cv, -1), spec(cv, 0), spec(cv, 1),
            pl.BlockSpec((None, 4, C_TQ, 3 * C_TQ), lambda b, i, g: (case(i), g, 0, 0)),
        ],
        out_specs=pl.BlockSpec((C_TQ, wb), lambda b, i, g: (b * nb + i, g)),
        out_shape=jax.ShapeDtypeStruct((batch * seq, MIXW), BF),
        compiler_params=_params("parallel", "parallel", "arbitrary"),
        name="attn_c",
    )(proj, proj, proj, proj, proj, proj, proj, bias)


def _merge_kernel(*refs, bounds):
    n = len(bounds)
    oas, obs, ocs = refs[:n], refs[n:2 * n], refs[2 * n:3 * n]
    wa_ref, wb_ref, wc_ref, ga_ref, gb_ref, gc_ref, o_ref = refs[3 * n:]

    def br(o, w, g):
        return g[...].astype(F32) * jnp.dot(o[...], w[...].astype(BF), preferred_element_type=F32)

    def merged(p):
        o_ref[...] = (br(oas[p], wa_ref, ga_ref) + br(obs[p], wb_ref, gb_ref) + br(ocs[p], wc_ref, gc_ref)
                      ).astype(o_ref.dtype)

    if n == 1:
        merged(0)
        return
    i = pl.program_id(0)
    for p, (b0, b1) in enumerate(bounds):
        @pl.when((i >= b0) & (i < b1))
        def _():
            merged(p)


def merge(oa_parts, ob_parts, oc_parts, wa, wb, wc, layer, proj, cfg):
    m = proj.shape[0]
    d = wa.shape[2]
    tm, tn = cfg.tm, cfg.tn
    g0 = QKV_COLS // tn
    gs = d // tn
    bounds = _row_parts(oa_parts, tm)

    def rowblks(parts):
        return [pl.BlockSpec((tm, x.shape[1]), lambda i, j, b0=b0, b1=b1: (_part_row(i, b0, b1), 0))
                for (x, _), (b0, b1) in zip(parts, bounds)]

    def wblk(k):
        return pl.BlockSpec((None, k, tn), lambda i, j: (layer, 0, j))

    def gate(n):
        return pl.BlockSpec((tm, tn), lambda i, j: (i, g0 + n * gs + j))

    return pl.pallas_call(
        functools.partial(_merge_kernel, bounds=bounds),
        grid=(m // tm, d // tn),
        in_specs=rowblks(oa_parts) + rowblks(ob_parts) + rowblks(oc_parts)
        + [wblk(wa.shape[1]), wblk(MIXW), wblk(MIXW), gate(0), gate(1), gate(2)],
        out_specs=pl.BlockSpec((tm, tn), lambda i, j: (i, j)),
        out_shape=jax.ShapeDtypeStruct((m, d), BF),
        compiler_params=_params("parallel", "arbitrary"),
        name="merge",
    )(*[x for x, _ in oa_parts + ob_parts + oc_parts], wa, wb, wc, proj, proj, proj)


def _out_proj_kernel(m_ref, w_ref, *refs, bounds, tile0):
    xs, o_ref = refs[:len(bounds)], refs[-1]
    acc = jnp.dot(m_ref[...], w_ref[...].astype(BF), preferred_element_type=F32)
    if len(bounds) == 1:
        o_ref[...] = xs[0][...] + acc
        return
    i = pl.program_id(0) + tile0
    for x_ref, (b0, b1) in zip(xs, bounds):
        @pl.when((i >= b0) & (i < b1))
        def _():
            o_ref[...] = x_ref[...] + acc


def out_proj(merged, w_out, layer, x_parts, cfg, out_parts=None):
    m, d = merged.shape
    tm, tn = cfg.tm, cfg.tn
    bounds = _row_parts(x_parts, tm)
    last = d // tn - 1
    outs = []
    for rows, off in (out_parts or [(m, 0)]):
        t0 = off // tm

        def residual(b0, b1, t0=t0):
            return pl.BlockSpec((tm, tn), lambda i, j: (_part_row(t0 + i, b0, b1),
                                                        jnp.where(t0 + i < b0, 0, jnp.where(t0 + i >= b1, last, j))))

        outs.append(pl.pallas_call(
            functools.partial(_out_proj_kernel, bounds=bounds, tile0=t0),
            grid=(rows // tm, d // tn),
            in_specs=[
                pl.BlockSpec((tm, d), lambda i, j, t0=t0: (t0 + i, 0)),
                pl.BlockSpec((None, d, tn), lambda i, j: (layer, 0, j)),
            ] + [residual(b0, b1) for b0, b1 in bounds],
            out_specs=pl.BlockSpec((tm, tn), lambda i, j: (i, j)),
            out_shape=jax.ShapeDtypeStruct((rows, d), F32),
            compiler_params=_params("parallel", "arbitrary"),
            name="out_proj",
        )(merged, w_out, *[x for x, _ in x_parts]))
    return outs if out_parts else outs[0]


def _router_kernel(*refs, bounds, n_experts):
    xs, g_ref, w_ref, o_ref = refs[:len(bounds)], refs[-3], refs[-2], refs[-1]

    def split(v):
        hi = v.astype(BF)
        return hi, (v - hi.astype(F32)).astype(BF)

    def affinities(x_ref):
        h_hi, h_lo = split(_rms(x_ref[...]) * g_ref[...])
        w_hi, w_lo = split(w_ref[...])
        mm = functools.partial(jnp.dot, preferred_element_type=F32)
        logits = mm(h_hi, w_hi) + (mm(h_hi, w_lo) + mm(h_lo, w_hi))
        lane = lax.broadcasted_iota(jnp.int32, logits.shape, 1)
        logits = jnp.where(lane < n_experts, logits, NEG)
        p = jnp.exp(logits - jnp.max(logits, axis=-1, keepdims=True))
        o_ref[...] = p / jnp.sum(p, axis=-1, keepdims=True)

    if len(bounds) == 1:
        affinities(xs[0])
        return
    i = pl.program_id(0)
    for x_ref, (b0, b1) in zip(xs, bounds):
        @pl.when((i >= b0) & (i < b1))
        def _():
            affinities(x_ref)


def router(x_parts, g, w_router, m, cfg):
    d = x_parts[0][0].shape[1]
    tm = min(512, cfg.tm)
    e = w_router.shape[1]
    wpad = jnp.zeros((d, HEAD), F32).at[:, :e].set(w_router)
    bounds = _row_parts(x_parts, tm)
    return pl.pallas_call(
        functools.partial(_router_kernel, bounds=bounds, n_experts=e),
        grid=(m // tm,),
        in_specs=[pl.BlockSpec((tm, d), lambda i, b0=b0, b1=b1: (_part_row(i, b0, b1), 0)) for b0, b1 in bounds]
        + [pl.BlockSpec((1, d), lambda i: (0, 0)), pl.BlockSpec((d, HEAD), lambda i: (0, 0))],
        out_specs=pl.BlockSpec((tm, HEAD), lambda i: (i, 0)),
        out_shape=jax.ShapeDtypeStruct((m, HEAD), F32),
        compiler_params=_params("parallel"),
        name="router",
    )(*[x for x, _ in x_parts], g.reshape(1, d), wpad)


def _prefix_incl(x01, upper):
    within = jnp.dot(x01.astype(BF), upper, preferred_element_type=F32)
    totals = jnp.broadcast_to(within[:, SEL_CHUNK - 1:SEL_CHUNK], within.shape)
    chunk = lax.broadcasted_iota(jnp.int32, within.shape, 0)
    before = jnp.zeros_like(within)
    for j in range(within.shape[0] - 1):
        before = before + jnp.where(chunk > j, totals[j:j + 1, :], 0.0)
    return within + before


def _topc_kernel(a_ref, tab_ref, idx_ref, gate_ref, *, cap, n_experts):
    a = a_ref[...]
    nch = a.shape[0]
    bits = pltpu.bitcast(a, jnp.int32)

    def count(mask):
        return jnp.sum(jnp.where(mask, 1.0, 0.0), keepdims=True)

    def keeps(cand):
        return jnp.where(count(bits >= cand) >= cap, 1, 0)

    def two_bits(i, t):
        shift = 28 - 2 * i
        digit = keeps(t | jnp.left_shift(jnp.int32(1), shift))
        for d in (2, 3):
            digit = digit + keeps(t | jnp.left_shift(jnp.int32(d), shift))
        return t | jnp.left_shift(digit, shift)

    top = jnp.left_shift(keeps(jnp.full((1, 1), 1 << 30, jnp.int32)), 30)
    thr = lax.fori_loop(0, 15, two_bits, top)
    r_i = lax.broadcasted_iota(jnp.int32, (SEL_CHUNK, SEL_CHUNK), 0)
    c_i = lax.broadcasted_iota(jnp.int32, (SEL_CHUNK, SEL_CHUNK), 1)
    upper = jnp.where(r_i <= c_i, 1.0, 0.0).astype(BF)

    gt = bits > thr
    eq = bits == thr
    room = cap - count(gt)
    eq_f = jnp.where(eq, 1.0, 0.0)
    eq_before = _prefix_incl(eq_f, upper) - eq_f
    sel = gt | (eq & (eq_before < room))
    rank = jnp.where(sel, _prefix_incl(jnp.where(sel, 1.0, 0.0), upper), 0.0)

    want = (lax.broadcasted_iota(jnp.int32, (cap, SEL_CHUNK), 0) + 1).astype(F32)
    acc = jnp.zeros((cap, HEAD), F32)
    for j in range(nch):
        hit = jnp.where(rank[j:j + 1, :] == want, 1.0, 0.0).astype(BF)
        acc = acc + jnp.dot(hit, tab_ref[j * SEL_CHUNK:(j + 1) * SEL_CHUNK, :], preferred_element_type=F32)
    lane = lax.broadcasted_iota(jnp.int32, acc.shape, 1)

    def pick(l):
        return jnp.sum(jnp.where(lane == l, acc, 0.0), axis=-1, keepdims=True)

    e = pl.program_id(0)
    gate_ref[...] = (pick(e) + pick(e + n_experts)) + pick(e + 2 * n_experts)
    idx_ref[...] = (pick(3 * n_experts + 1) * SEL_CHUNK + pick(3 * n_experts)).astype(jnp.int32)


def topc_table(aff):
    n, e = aff.shape
    assert 3 * e + 2 <= HEAD and n // SEL_CHUNK <= 256
    hi = aff.astype(BF)
    rest = aff - hi.astype(F32)
    mid = rest.astype(BF)
    lo = (rest - mid.astype(F32)).astype(BF)
    tok = jnp.arange(n, dtype=jnp.int32)
    pos = (tok % SEL_CHUNK).astype(BF)[:, None]
    chunk = (tok // SEL_CHUNK).astype(BF)[:, None]
    return jnp.concatenate([hi, mid, lo, pos, chunk, jnp.zeros((n, HEAD - 3 * e - 2), BF)], axis=1)


def topc(aff, cap):
    n, e = aff.shape
    nch = n // SEL_CHUNK
    return pl.pallas_call(
        functools.partial(_topc_kernel, cap=cap, n_experts=e),
        grid=(e,),
        in_specs=[pl.BlockSpec((None, nch, SEL_CHUNK), lambda i: (i, 0, 0)),
                  pl.BlockSpec((n, HEAD), lambda i: (0, 0))],
        out_specs=[pl.BlockSpec((None, cap, 1), lambda i: (i, 0, 0)), pl.BlockSpec((None, cap, 1), lambda i: (i, 0, 0))],
        out_shape=[jax.ShapeDtypeStruct((e, cap, 1), jnp.int32), jax.ShapeDtypeStruct((e, cap, 1), F32)],
        compiler_params=_params("parallel"),
        name="topc",
    )(aff.T.reshape(e, nch, SEL_CHUNK), topc_table(aff))


def _row_copy(src_hbm, dst, sem, tok, r):
    return pltpu.make_async_copy(src_hbm.at[pl.ds(tok, 1), :], dst.at[pl.ds(r, 1), :], sem)


ROW_LOOP_UNROLL = 8


def _gather_norm_kernel(idx_ref, *refs, tr, offs, tiles_per_list):
    srcs, (g_ref, o_ref, buf, sem) = refs[:len(offs)], refs[len(offs):]
    i = pl.program_id(0)
    slot = i % 2

    def rows(tile, s, go):
        def loop(src, off):
            def body(r, c):
                cp = _row_copy(src, buf.at[s], sem.at[s], idx_ref[tile * tr + r] - off, r)
                cp.start() if go else cp.wait()
                return c
            lax.fori_loop(0, tr, body, 0, unroll=ROW_LOOP_UNROLL)

        if len(srcs) == 1:
            loop(srcs[0], offs[0])
            return
        group = (tile // tiles_per_list) % len(srcs)
        for p, (src, off) in enumerate(zip(srcs, offs)):
            @pl.when(group == p)
            def _():
                loop(src, off)

    @pl.when(i == 0)
    def _():
        rows(0, 0, True)

    @pl.when(i + 1 < pl.num_programs(0))
    def _():
        rows(i + 1, 1 - slot, True)

    rows(i, slot, False)
    o_ref[...] = (_rms(buf[slot]) * g_ref[...]).astype(o_ref.dtype)


def gather_norm(x_parts, g, idx, cap, cfg):
    d = x_parts[0][0].shape[1]
    r = idx.shape[0]
    tr = cfg.tr
    assert cap % tr == 0 and len(x_parts) in (1, len(cfg.groups))
    return pl.pallas_call(
        functools.partial(_gather_norm_kernel, tr=tr, offs=tuple(o for _, o in x_parts), tiles_per_list=cap // tr),
        grid_spec=pltpu.PrefetchScalarGridSpec(
            num_scalar_prefetch=1,
            grid=(r // tr,),
            in_specs=[pl.BlockSpec(memory_space=pl.ANY)] * len(x_parts) + [pl.BlockSpec((1, d), lambda i, idx: (0, 0))],
            out_specs=pl.BlockSpec((tr, d), lambda i, idx: (i, 0)),
            scratch_shapes=[pltpu.VMEM((2, tr, d), F32), pltpu.SemaphoreType.DMA((2,))],
        ),
        out_shape=jax.ShapeDtypeStruct((r, d), BF),
        compiler_params=_params("arbitrary"),
        name="gather_norm",
    )(idx, *[x for x, _ in x_parts], g.reshape(1, d))


def _ffn_up_kernel(x_ref, wg_ref, wu_ref, o_ref):
    x = x_ref[...]
    gate = jnp.dot(x, wg_ref[...].astype(BF), preferred_element_type=F32)
    up = jnp.dot(x, wu_ref[...].astype(BF), preferred_element_type=F32)
    o_ref[...] = (jax.nn.silu(gate) * up).astype(o_ref.dtype)


def ffn_up(xe, w_gate, w_up, layer, cfg):
    _, e, d, f = w_gate.shape
    rows = xe.shape[0] // e
    tf = cfg.tf
    return pl.pallas_call(
        _ffn_up_kernel,
        grid=(e, f // tf),
        in_specs=[
            pl.BlockSpec((rows, d), lambda i, j: (i, 0), pipeline_mode=pl.Buffered(1)),
            pl.BlockSpec((None, None, d, tf), lambda i, j: (layer, i, 0, j)),
            pl.BlockSpec((None, None, d, tf), lambda i, j: (layer, i, 0, j)),
        ],
        out_specs=pl.BlockSpec((rows, tf), lambda i, j: (i, j)),
        out_shape=jax.ShapeDtypeStruct((xe.shape[0], f), BF),
        compiler_params=_params("parallel", "arbitrary"),
        name="ffn_up",
    )(xe, w_gate, w_up)


def _ffn_down_kernel(h_ref, w_ref, g_ref, o_ref):
    o_ref[...] = jnp.dot(h_ref[...], w_ref[...].astype(BF), preferred_element_type=F32) * g_ref[...]


def ffn_down(hid, w_down, layer, gate, cfg):
    _, e, f, d = w_down.shape
    rows = hid.shape[0] // e
    tn = cfg.tn
    return pl.pallas_call(
        _ffn_down_kernel,
        grid=(e, d // tn),
        in_specs=[
            pl.BlockSpec((rows, f), lambda i, j: (i, 0)),
            pl.BlockSpec((None, None, f, tn), lambda i, j: (layer, i, 0, j)),
            pl.BlockSpec((rows, 1), lambda i, j: (i, 0)),
        ],
        out_specs=pl.BlockSpec((rows, tn), lambda i, j: (i, j)),
        out_shape=jax.ShapeDtypeStruct((hid.shape[0], d), F32),
        compiler_params=_params("parallel", "arbitrary"),
        name="ffn_down",
    )(hid, w_down, gate)


def _scatter_add_kernel(idx_ref, y_ref, x_hbm, o_hbm, buf, sem, *, tr, off, tile_of, tiles_per_expert):
    del x_hbm
    i = pl.program_id(0)
    tile = tile_of(i)
    slot = i % 2
    pos = i % tiles_per_expert
    first = pos == 0
    last = pos == tiles_per_expert - 1

    def fetch(t, s, r):
        tok = idx_ref[t * tr + r] - off
        return pltpu.make_async_copy(o_hbm.at[pl.ds(tok, 1), :], buf.at[s, pl.ds(r, 1), :], sem.at[s])

    def store(t, s, r):
        tok = idx_ref[t * tr + r] - off
        return pltpu.make_async_copy(buf.at[s, pl.ds(r, 1), :], o_hbm.at[pl.ds(tok, 1), :], sem.at[2 + s])

    def each(fn):
        def body(r, c):
            fn(r)
            return c
        lax.fori_loop(0, tr, body, 0, unroll=ROW_LOOP_UNROLL)

    @pl.when(first)
    def _():
        each(lambda r: fetch(tile, slot, r).start())

    @pl.when(jnp.logical_not(first))
    def _():
        each(lambda r: store(tile - 1, 1 - slot, r).wait())

    @pl.when(jnp.logical_not(last))
    def _():
        each(lambda r: fetch(tile + 1, 1 - slot, r).start())

    each(lambda r: fetch(tile, slot, r).wait())
    buf[slot] = buf[slot] + y_ref[...]
    each(lambda r: store(tile, slot, r).start())

    @pl.when(last)
    def _():
        each(lambda r: store(tile, slot, r).wait())


def scatter_add(x_parts, ye, idx, cap, cfg):
    d = ye.shape[1]
    r = idx.shape[0]
    tr = cfg.tr
    n_parts = len(x_parts)
    per_expert = r // cfg.n_experts // tr
    mine = per_expert // n_parts
    assert cap % tr == 0 and n_parts in (1, len(cfg.groups)) and mine * n_parts == per_expert
    outs = []
    for p, (x, off) in enumerate(x_parts):
        def tile_of(i, p=p):
            return (i // mine) * per_expert + p * mine + i % mine

        outs.append(pl.pallas_call(
            functools.partial(_scatter_add_kernel, tr=tr, off=off, tile_of=tile_of, tiles_per_expert=mine),
            grid_spec=pltpu.PrefetchScalarGridSpec(
                num_scalar_prefetch=1,
                grid=(cfg.n_experts * mine,),
                in_specs=[pl.BlockSpec((tr, d), lambda i, idx, tile_of=tile_of: (tile_of(i), 0)),
                          pl.BlockSpec(memory_space=pl.ANY)],
                out_specs=pl.BlockSpec(memory_space=pl.ANY),
                scratch_shapes=[pltpu.VMEM((2, tr, d), F32), pltpu.SemaphoreType.DMA((4,))],
            ),
            out_shape=jax.ShapeDtypeStruct(x.shape, F32),
            input_output_aliases={2: 0},
            compiler_params=_params("arbitrary"),
            name="scatter_add",
        )(idx, ye, x))
    return outs


def _rope_tables(cfg):
    inv = 1.0 / (ROPE_THETA ** (jnp.arange(0, HEAD, 2, dtype=F32) / HEAD))
    cos, sin = [], []
    sign = jnp.where(jnp.arange(HEAD) < HEAD // 2, -1.0, 1.0).astype(F32)
    for batch, seq in cfg.groups:
        ang = jnp.arange(seq, dtype=F32)[:, None] * inv[None, :]
        ang = jnp.concatenate([ang, ang], axis=-1)
        cos.append(jnp.tile(jnp.cos(ang), (batch, 1)))
        sin.append(jnp.tile(jnp.sin(ang) * sign, (batch, 1)))
    return jnp.concatenate(cos, axis=0), jnp.concatenate(sin, axis=0)


def _group_offsets(cfg):
    offs, o = [], 0
    for batch, seq in cfg.groups:
        offs.append(o)
        o += batch * seq
    return offs, o


def _mixer(x_parts, l, cos, sin, p, cfg, out_parts):
    d = cfg.d_model
    offs, m = _group_offsets(cfg)
    ones = jnp.ones((MIXW,), F32)
    heads = MIXW // HEAD
    qk = p["qk_norm"][l].astype(F32)
    nw = jnp.concatenate([
        jnp.tile(qk[0] * Q_SCALE, heads), jnp.tile(qk[1], heads), ones,
        jnp.tile(qk[2] * Q_SCALE, heads), jnp.tile(qk[3], heads), ones,
        jnp.tile(qk[4] * Q_SCALE, heads), jnp.tile(qk[5], heads), ones,
        jnp.ones((3 * d,), F32)]).reshape(1, -1)
    bias = jnp.concatenate([jnp.zeros((QKV_COLS,), F32), p["b_gate"][l].astype(F32)]).reshape(1, -1)

    h = rmsnorm_bf16(x_parts, p["g_mix"][l], m, min(512, cfg.tm))
    proj = in_proj(h, p["w_in"], l, nw, bias, cos, sin, cfg)

    lam_init = 0.8 - 0.6 * math.exp(-0.3 * l)
    cbias = natten_bias(p["rpb"][l].astype(F32))
    oa, ob, oc = [], [], []
    for (batch, seq), goff in zip(cfg.groups, offs):
        oa.append((attn_a(proj, goff, batch, seq, cfg), goff))
        ob.append((attn_b(proj, p["lambda_qk"][l].astype(F32), p["subln"][l].astype(F32), lam_init, goff, batch, seq,
                          cfg), goff))
        oc.append((attn_c(proj, cbias, goff, batch, seq, cfg), goff))
    merged = merge(oa, ob, oc, p["w_br_a"], p["w_br_b"], p["w_br_c"], l, proj, cfg)
    new = out_proj(merged, p["w_out"], l, x_parts, cfg, out_parts)
    return [(x, off) for x, (_, off) in zip(new, out_parts)] if out_parts else [(new, 0)]


def _moe(x_parts, l, p, cfg):
    e = cfg.n_experts
    offs, m = _group_offsets(cfg)
    aff = router(x_parts, p["g_ffn"][l], p["w_router"][l], m, cfg)[:, :e]
    idx, gate = [], []
    caps = {CAP_FACTOR * batch * seq // e for batch, seq in cfg.groups}
    assert len(caps) == 1
    cap = caps.pop()
    for (batch, seq), goff in zip(cfg.groups, offs):
        n = batch * seq
        i_g, g_g = topc(aff[goff:goff + n], cap)
        idx.append(i_g.reshape(e, cap) + goff)
        gate.append(g_g.reshape(e, cap))
    idx = jnp.concatenate(idx, axis=1).reshape(-1)
    gate = jnp.concatenate(gate, axis=1).reshape(-1, 1)
    xe = gather_norm(x_parts, p["g_ffn"][l], idx, cap, cfg)
    hid = ffn_up(xe, p["w_e_gate"], p["w_e_up"], l, cfg)
    ye = ffn_down(hid, p["w_e_down"], l, gate, cfg)
    return [(x, off) for x, (_, off) in zip(scatter_add(x_parts, ye, idx, cap, cfg), x_parts)]


def trunk(xs, p, cfg):
    d = cfg.d_model
    offs, _ = _group_offsets(cfg)
    x_parts = [(a.reshape(-1, d), o) for a, o in zip(xs, offs)]
    by_group = [(b * s, o) for (b, s), o in zip(cfg.groups, offs)]
    cos, sin = _rope_tables(cfg)
    for l in range(cfg.depth):
        out_parts = by_group if l == cfg.depth - 1 else None
        x_parts = _moe(_mixer(x_parts, l, cos, sin, p, cfg, out_parts), l, p, cfg)
    return tuple(x.reshape(b, s, d) for (x, _), (b, s) in zip(x_parts, cfg.groups))


def kernel(x_prompt, x_sample, g_mix, w_in, b_gate, qk_norm, lambda_qk, subln, rpb, w_br_a, w_br_b, w_br_c,
           w_out, g_ffn, w_router, w_e_gate, w_e_up, w_e_down):
    cfg = Cfg(d_model=x_prompt.shape[-1],
              groups=(x_prompt.shape[:2], x_sample.shape[:2]),
              n_experts=w_router.shape[-1], d_ff=w_e_gate.shape[-1], depth=g_mix.shape[0])
    p = dict(g_mix=g_mix, w_in=w_in, b_gate=b_gate, qk_norm=qk_norm, lambda_qk=lambda_qk, subln=subln, rpb=rpb,
             w_br_a=w_br_a, w_br_b=w_br_b, w_br_c=w_br_c, w_out=w_out, g_ffn=g_ffn, w_router=w_router,
             w_e_gate=w_e_gate, w_e_up=w_e_up, w_e_down=w_e_down)
    return trunk((x_prompt, x_sample), p, cfg)
```

```python
import functools
import math
from typing import NamedTuple

import numpy as np
import jax
import jax.numpy as jnp
from jax import lax
from jax.experimental import pallas as pl
from jax.experimental.pallas import tpu as pltpu

HEAD = 128
MIXW = 1536
QKV_COLS = 9 * MIXW
A_GROUPS = ((64, 1), (256, 4), (1024, 16))
A_HPG = 4
B_HEADS = 6
C_HEADS = 12
GRID_W = 64
NA_ROWS = 8
NA_COLS = 16
C_ROWS_PER_BLOCK = 4
C_TQ = C_ROWS_PER_BLOCK * GRID_W
EPS = 1e-6
NEG = -1e30
LOG2E = math.log2(math.e)
Q_SCALE = HEAD ** -0.5 * LOG2E
ROPE_THETA = 10000.0
CAP_FACTOR = 2
SEL_CHUNK = 256
VMEM_LIMIT = 56 * 1024 * 1024

BF = jnp.bfloat16
F32 = jnp.float32


class Cfg(NamedTuple):
    d_model: int
    groups: tuple
    n_experts: int
    d_ff: int
    depth: int
    tm: int = 1024
    tn: int = 512
    tq_a: int = 128
    tq_b: int = 1024
    sub_b: int = 256
    tr: int = 256
    tf: int = 256


def _params(*sem):
    return pltpu.CompilerParams(dimension_semantics=sem, vmem_limit_bytes=VMEM_LIMIT)


def _rms(x):
    return x * lax.rsqrt(jnp.mean(x * x, axis=-1, keepdims=True) + EPS)


def _dot_t(a, b):
    return lax.dot_general(a, b, (((1,), (1,)), ((), ())), preferred_element_type=F32)


def _row_parts(parts, tm):
    return [(off // tm, (off + x.shape[0]) // tm) for x, off in parts]


def _part_row(i, b0, b1):
    return jnp.clip(i - b0, 0, b1 - b0 - 1)


def _rmsnorm_kernel(*refs, bounds):
    xs, g_ref, o_ref = refs[:len(bounds)], refs[-2], refs[-1]
    if len(bounds) == 1:
        o_ref[...] = (_rms(xs[0][...]) * g_ref[...]).astype(o_ref.dtype)
        return
    i = pl.program_id(0)
    for x_ref, (b0, b1) in zip(xs, bounds):
        @pl.when((i >= b0) & (i < b1))
        def _():
            o_ref[...] = (_rms(x_ref[...]) * g_ref[...]).astype(o_ref.dtype)


def rmsnorm_bf16(parts, g, m, tm):
    d = parts[0][0].shape[1]
    bounds = _row_parts(parts, tm)
    return pl.pallas_call(
        functools.partial(_rmsnorm_kernel, bounds=bounds),
        grid=(m // tm,),
        in_specs=[pl.BlockSpec((tm, d), lambda i, b0=b0, b1=b1: (_part_row(i, b0, b1), 0)) for b0, b1 in bounds]
        + [pl.BlockSpec((1, d), lambda i: (0, 0))],
        out_specs=pl.BlockSpec((tm, d), lambda i: (i, 0)),
        out_shape=jax.ShapeDtypeStruct((m, d), BF),
        compiler_params=_params("parallel"),
        name="rmsnorm",
    )(*[x for x, _ in parts], g.reshape(1, d))


def _in_proj_kernel(h_ref, w_ref, nw_ref, b_ref, cos_ref, sin_ref, o_ref, *, tn):
    seg = (pl.program_id(1) * tn) // MIXW
    acc = jnp.dot(h_ref[...], w_ref[...].astype(BF), preferred_element_type=F32)
    heads = tn // HEAD

    def normed(h):
        sl = slice(h * HEAD, (h + 1) * HEAD)
        return _rms(acc[:, sl]) * nw_ref[:, sl]

    @pl.when((seg == 0) | (seg == 1) | (seg == 3) | (seg == 4))
    def _():
        for h in range(heads):
            y = normed(h)
            y = y * cos_ref[...] + pltpu.roll(y, HEAD // 2, 1) * sin_ref[...]
            o_ref[:, h * HEAD:(h + 1) * HEAD] = y.astype(o_ref.dtype)

    @pl.when((seg == 6) | (seg == 7))
    def _():
        for h in range(heads):
            o_ref[:, h * HEAD:(h + 1) * HEAD] = normed(h).astype(o_ref.dtype)

    @pl.when((seg == 2) | (seg == 5) | (seg == 8))
    def _():
        o_ref[...] = acc.astype(o_ref.dtype)

    @pl.when(seg >= 9)
    def _():
        o_ref[...] = (0.5 * jnp.tanh(0.5 * (acc + b_ref[...])) + 0.5).astype(o_ref.dtype)


def in_proj(h, w_in, layer, nw, bias, cos, sin_signed, cfg):
    m, d = h.shape
    ncol = w_in.shape[2]
    tm, tn = cfg.tm, cfg.tn
    return pl.pallas_call(
        functools.partial(_in_proj_kernel, tn=tn),
        grid=(m // tm, ncol // tn),
        in_specs=[
            pl.BlockSpec((tm, d), lambda i, j: (i, 0)),
            pl.BlockSpec((None, d, tn), lambda i, j: (layer, 0, j)),
            pl.BlockSpec((1, tn), lambda i, j: (0, j)),
            pl.BlockSpec((1, tn), lambda i, j: (0, j)),
            pl.BlockSpec((tm, HEAD), lambda i, j: (i, 0)),
            pl.BlockSpec((tm, HEAD), lambda i, j: (i, 0)),
        ],
        out_specs=pl.BlockSpec((tm, tn), lambda i, j: (i, j)),
        out_shape=jax.ShapeDtypeStruct((m, ncol), BF),
        compiler_params=_params("parallel", "arbitrary"),
        name="in_proj",
    )(h, w_in, nw, bias, cos, sin_signed)


def _attn_a_kernel(q_ref, k_ref, v_ref, o_ref, *, tq, seq):
    t0 = pl.program_id(1) * tq
    geo = []
    for radius, dil in A_GROUPS:
        width = min(tq + 2 * radius, seq)
        start = pl.multiple_of(jnp.clip(t0 - radius, 0, seq - width), 64)
        diff = (start - t0) + lax.broadcasted_iota(jnp.int32, (tq, width), 1) \
            - lax.broadcasted_iota(jnp.int32, (tq, width), 0)
        valid = (jnp.abs(diff) <= radius) & ((diff & (dil - 1)) == 0)
        geo.append((start, width, jnp.where(valid, 0.0, NEG)))
    for hh in range(A_HPG):
        scores = []
        for g, (start, width, mask) in enumerate(geo):
            col = (g * A_HPG + hh) * HEAD
            q = q_ref[:, col:col + HEAD]
            k = k_ref[pl.ds(start, width), col:col + HEAD]
            scores.append(_dot_t(q, k) + mask)
        mx = functools.reduce(jnp.maximum, [jnp.max(s, axis=-1, keepdims=True) for s in scores])
        den = jnp.zeros((tq, 1), F32)
        acc = jnp.zeros((tq, HEAD), F32)
        for g, (start, width, mask) in enumerate(geo):
            col = (g * A_HPG + hh) * HEAD
            p = jnp.exp2(scores[g] - mx)
            den = den + jnp.sum(p, axis=-1, keepdims=True)
            v = v_ref[pl.ds(start, width), col:col + HEAD]
            acc = acc + jnp.dot(p.astype(BF), v, preferred_element_type=F32)
        o_ref[:, hh * HEAD:(hh + 1) * HEAD] = (acc / den).astype(o_ref.dtype)


def attn_a(proj, goff, batch, seq, cfg):
    tq = cfg.tq_a
    rb0, sb0 = goff // tq, goff // seq
    nq = seq // tq
    whole = dict(pipeline_mode=pl.Buffered(1))
    return pl.pallas_call(
        functools.partial(_attn_a_kernel, tq=tq, seq=seq),
        grid=(batch, nq),
        in_specs=[
            pl.BlockSpec((tq, MIXW), lambda b, i: (rb0 + b * nq + i, 0)),
            pl.BlockSpec((seq, MIXW), lambda b, i: (sb0 + b, 1), **whole),
            pl.BlockSpec((seq, MIXW), lambda b, i: (sb0 + b, 2), **whole),
        ],
        out_specs=pl.BlockSpec((tq, A_HPG * HEAD), lambda b, i: (b * nq + i, 0)),
        out_shape=jax.ShapeDtypeStruct((batch * seq, A_HPG * HEAD), BF),
        compiler_params=_params("parallel", "arbitrary"),
        name="attn_a",
    )(proj, proj, proj)


def _attn_b_kernel(q_ref, k_ref, v_ref, lam_ref, g_ref, o_ref, s_even, s_odd, *, lam_init, sub):
    n = q_ref.shape[0] // sub
    lq = lam_ref[...]
    lam = (jnp.exp(jnp.sum(lq[0:1] * lq[1:2], keepdims=True))
           - jnp.exp(jnp.sum(lq[2:3] * lq[3:4], keepdims=True)) + lam_init)

    def rows_of(i):
        return pl.ds(pl.multiple_of(i * sub, sub), sub)

    def scores(i, dst):
        for c in range(2):
            dst[c] = _dot_t(q_ref[rows_of(i), c * HEAD:(c + 1) * HEAD], k_ref[:, c * HEAD:(c + 1) * HEAD])

    def finish(i, src):
        def unnormalised(c):
            s = src[c]
            p = jnp.exp2(s - jnp.max(s, axis=-1, keepdims=True))
            return p, 1.0 / jnp.sum(p, axis=-1, keepdims=True)

        p0, r0 = unnormalised(0)
        p1, r1 = unnormalised(1)
        a = (p0 * r0 - p1 * (lam * r1)).astype(BF)
        o = jnp.dot(a, v_ref[...], preferred_element_type=F32)
        o_ref[rows_of(i), :] = ((_rms(o) * g_ref[...]) * (1.0 - lam_init)).astype(o_ref.dtype)

    scores(0, s_even)

    def pair(j, carry):
        i = 2 * j
        scores(i + 1, s_odd)
        finish(i, s_even)
        scores(i + 2, s_even)
        finish(i + 1, s_odd)
        return carry

    lax.fori_loop(0, n // 2 - 1, pair, 0)
    scores(n - 1, s_odd)
    finish(n - 2, s_even)
    finish(n - 1, s_odd)


def attn_b(proj, lambda_qk, subln, lam_init, goff, batch, seq, cfg):
    tq = min(cfg.tq_b, seq)
    sub = cfg.sub_b
    assert tq % (2 * sub) == 0
    w = 2 * HEAD
    rb0, sb0 = goff // tq, goff // seq
    nq = seq // tq
    c0 = 3 * MIXW // w
    return pl.pallas_call(
        functools.partial(_attn_b_kernel, lam_init=lam_init, sub=sub),
        grid=(batch, B_HEADS, nq),
        in_specs=[
            pl.BlockSpec((tq, w), lambda b, h, i: (rb0 + b * nq + i, c0 + h)),
            pl.BlockSpec((seq, w), lambda b, h, i: (sb0 + b, c0 + B_HEADS + h)),
            pl.BlockSpec((seq, w), lambda b, h, i: (sb0 + b, c0 + 2 * B_HEADS + h)),
            pl.BlockSpec((4, HEAD), lambda b, h, i: (0, 0)),
            pl.BlockSpec((1, w), lambda b, h, i: (0, 0)),
        ],
        out_specs=pl.BlockSpec((tq, w), lambda b, h, i: (b * nq + i, h)),
        out_shape=jax.ShapeDtypeStruct((batch * seq, MIXW), BF),
        scratch_shapes=[pltpu.VMEM((2, sub, seq), F32), pltpu.VMEM((2, sub, seq), F32)],
        compiler_params=_params("parallel", "parallel", "arbitrary"),
        name="attn_b",
    )(proj, proj, proj, lambda_qk, subln.reshape(1, w))


def _natten_bias_index():
    j = np.arange(GRID_W)
    toeplitz = np.clip(j[None, :] - j[:, None], -(NA_COLS - 1), NA_COLS - 1) + NA_COLS - 1
    cs = np.clip(j - NA_COLS // 2, 0, GRID_W - NA_COLS)
    col_ok = (j[None, :] >= cs[:, None]) & (j[None, :] < cs[:, None] + NA_COLS)
    a = np.arange(C_ROWS_PER_BLOCK)
    kb = np.arange(3 * C_ROWS_PER_BLOCK)
    drow = (kb[None, :] - C_ROWS_PER_BLOCK) - a[:, None]
    kc = kb // C_ROWS_PER_BLOCK - 1
    row_ok = np.stack([
        np.broadcast_to((kc >= 0)[None, :], drow.shape),
        (drow >= -(NA_ROWS // 2)) & (drow < NA_ROWS - NA_ROWS // 2),
        np.broadcast_to((kc <= 0)[None, :], drow.shape),
    ])
    ok = row_ok[:, :, None, :, None] & col_ok[None, None, :, None, :]
    rpb_row = np.clip(drow + NA_ROWS - 1, 0, 2 * NA_ROWS - 2)
    return toeplitz.astype(np.int32), rpb_row.astype(np.int32), ok.reshape(3, C_TQ, 3 * C_TQ)


def natten_bias(rpb):
    toeplitz, rpb_row, ok = _natten_bias_index()
    t = rpb[:, :, toeplitz]
    v = t[:, rpb_row]
    v = v.transpose(0, 1, 3, 2, 4).reshape(C_HEADS, C_TQ, 3 * C_TQ)
    return jnp.where(ok[:, None], v[None] * LOG2E, NEG).astype(F32)


def _attn_c_kernel(q_ref, kp_ref, ko_ref, kn_ref, vp_ref, vo_ref, vn_ref, bias_ref, o_ref):
    heads = o_ref.shape[1] // HEAD
    for h in range(heads):
        sl = slice(h * HEAD, (h + 1) * HEAD)
        k = jnp.concatenate([kp_ref[:, sl], ko_ref[:, sl], kn_ref[:, sl]], axis=0)
        v = jnp.concatenate([vp_ref[:, sl], vo_ref[:, sl], vn_ref[:, sl]], axis=0)
        s = _dot_t(q_ref[:, sl], k) + bias_ref[h]
        p = jnp.exp2(s - jnp.max(s, axis=-1, keepdims=True))
        o = jnp.dot(p.astype(BF), v, preferred_element_type=F32) / jnp.sum(p, axis=-1, keepdims=True)
        o_ref[:, sl] = o.astype(o_ref.dtype)


def attn_c(proj, bias, goff, batch, seq, cfg):
    wb = 4 * HEAD
    nhg = MIXW // wb
    nb = seq // C_TQ
    assert nb >= 3
    rb0 = goff // C_TQ
    cq, ck, cv = 6 * nhg, 7 * nhg, 8 * nhg

    def rows(b, i):
        return rb0 + b * nb + i

    def case(i):
        return jnp.where(i == 0, 0, jnp.where(i == nb - 1, 2, 1))

    def spec(c0, shift):
        return pl.BlockSpec((C_TQ, wb), lambda b, i, g: (rows(b, jnp.clip(i + shift, 0, nb - 1)), c0 + g))

    return pl.pallas_call(
        _attn_c_kernel,
        grid=(batch, nb, nhg),
        in_specs=[
            spec(cq, 0), spec(ck, -1), spec(ck, 0), spec(ck, 1), spec(cv, -1), spec(cv, 0), spec(cv, 1),
            pl.BlockSpec((None, 4, C_TQ, 3 * C_TQ), lambda b, i, g: (case(i), g, 0, 0)),
        ],
        out_specs=pl.BlockSpec((C_TQ, wb), lambda b, i, g: (b * nb + i, g)),
        out_shape=jax.ShapeDtypeStruct((batch * seq, MIXW), BF),
        compiler_params=_params("parallel", "parallel", "arbitrary"),
        name="attn_c",
    )(proj, proj, proj, proj, proj, proj, proj, bias)


def _merge_kernel(*refs, bounds):
    n = len(bounds)
    oas, obs, ocs = refs[:n], refs[n:2 * n], refs[2 * n:3 * n]
    wa_ref, wb_ref, wc_ref, ga_ref, gb_ref, gc_ref, o_ref = refs[3 * n:]

    def br(o, w, g):
        return g[...].astype(F32) * jnp.dot(o[...], w[...].astype(BF), preferred_element_type=F32)

    def merged(p):
        o_ref[...] = (br(oas[p], wa_ref, ga_ref) + br(obs[p], wb_ref, gb_ref) + br(ocs[p], wc_ref, gc_ref)
                      ).astype(o_ref.dtype)

    if n == 1:
        merged(0)
        return
    i = pl.program_id(0)
    for p, (b0, b1) in enumerate(bounds):
        @pl.when((i >= b0) & (i < b1))
        def _():
            merged(p)


def merge(oa_parts, ob_parts, oc_parts, wa, wb, wc, layer, proj, cfg):
    m = proj.shape[0]
    d = wa.shape[2]
    tm, tn = cfg.tm, cfg.tn
    g0 = QKV_COLS // tn
    gs = d // tn
    bounds = _row_parts(oa_parts, tm)

    def rowblks(parts):
        return [pl.BlockSpec((tm, x.shape[1]), lambda i, j, b0=b0, b1=b1: (_part_row(i, b0, b1), 0))
                for (x, _), (b0, b1) in zip(parts, bounds)]

    def wblk(k):
        return pl.BlockSpec((None, k, tn), lambda i, j: (layer, 0, j))

    def gate(n):
        return pl.BlockSpec((tm, tn), lambda i, j: (i, g0 + n * gs + j))

    return pl.pallas_call(
        functools.partial(_merge_kernel, bounds=bounds),
        grid=(m // tm, d // tn),
        in_specs=rowblks(oa_parts) + rowblks(ob_parts) + rowblks(oc_parts)
        + [wblk(wa.shape[1]), wblk(MIXW), wblk(MIXW), gate(0), gate(1), gate(2)],
        out_specs=pl.BlockSpec((tm, tn), lambda i, j: (i, j)),
        out_shape=jax.ShapeDtypeStruct((m, d), BF),
        compiler_params=_params("parallel", "arbitrary"),
        name="merge",
    )(*[x for x, _ in oa_parts + ob_parts + oc_parts], wa, wb, wc, proj, proj, proj)


def _out_proj_kernel(m_ref, w_ref, *refs, bounds, tile0):
    xs, o_ref = refs[:len(bounds)], refs[-1]
    acc = jnp.dot(m_ref[...], w_ref[...].astype(BF), preferred_element_type=F32)
    if len(bounds) == 1:
        o_ref[...] = xs[0][...] + acc
        return
    i = pl.program_id(0) + tile0
    for x_ref, (b0, b1) in zip(xs, bounds):
        @pl.when((i >= b0) & (i < b1))
        def _():
            o_ref[...] = x_ref[...] + acc


def out_proj(merged, w_out, layer, x_parts, cfg, out_parts=None):
    m, d = merged.shape
    tm, tn = cfg.tm, cfg.tn
    bounds = _row_parts(x_parts, tm)
    last = d // tn - 1
    outs = []
    for rows, off in (out_parts or [(m, 0)]):
        t0 = off // tm

        def residual(b0, b1, t0=t0):
            return pl.BlockSpec((tm, tn), lambda i, j: (_part_row(t0 + i, b0, b1),
                                                        jnp.where(t0 + i < b0, 0, jnp.where(t0 + i >= b1, last, j))))

        outs.append(pl.pallas_call(
            functools.partial(_out_proj_kernel, bounds=bounds, tile0=t0),
            grid=(rows // tm, d // tn),
            in_specs=[
                pl.BlockSpec((tm, d), lambda i, j, t0=t0: (t0 + i, 0)),
                pl.BlockSpec((None, d, tn), lambda i, j: (layer, 0, j)),
            ] + [residual(b0, b1) for b0, b1 in bounds],
            out_specs=pl.BlockSpec((tm, tn), lambda i, j: (i, j)),
            out_shape=jax.ShapeDtypeStruct((rows, d), F32),
            compiler_params=_params("parallel", "arbitrary"),
            name="out_proj",
        )(merged, w_out, *[x for x, _ in x_parts]))
    return outs if out_parts else outs[0]


def _router_kernel(*refs, bounds, n_experts):
    xs, g_ref, w_ref, o_ref = refs[:len(bounds)], refs[-3], refs[-2], refs[-1]

    def split(v):
        hi = v.astype(BF)
        return hi, (v - hi.astype(F32)).astype(BF)

    def affinities(x_ref):
        h_hi, h_lo = split(_rms(x_ref[...]) * g_ref[...])
        w_hi, w_lo = split(w_ref[...])
        mm = functools.partial(jnp.dot, preferred_element_type=F32)
        logits = mm(h_hi, w_hi) + (mm(h_hi, w_lo) + mm(h_lo, w_hi))
        lane = lax.broadcasted_iota(jnp.int32, logits.shape, 1)
        logits = jnp.where(lane < n_experts, logits, NEG)
        p = jnp.exp(logits - jnp.max(logits, axis=-1, keepdims=True))
        o_ref[...] = p / jnp.sum(p, axis=-1, keepdims=True)

    if len(bounds) == 1:
        affinities(xs[0])
        return
    i = pl.program_id(0)
    for x_ref, (b0, b1) in zip(xs, bounds):
        @pl.when((i >= b0) & (i < b1))
        def _():
            affinities(x_ref)


def router(x_parts, g, w_router, m, cfg):
    d = x_parts[0][0].shape[1]
    tm = min(512, cfg.tm)
    e = w_router.shape[1]
    wpad = jnp.zeros((d, HEAD), F32).at[:, :e].set(w_router)
    bounds = _row_parts(x_parts, tm)
    return pl.pallas_call(
        functools.partial(_router_kernel, bounds=bounds, n_experts=e),
        grid=(m // tm,),
        in_specs=[pl.BlockSpec((tm, d), lambda i, b0=b0, b1=b1: (_part_row(i, b0, b1), 0)) for b0, b1 in bounds]
        + [pl.BlockSpec((1, d), lambda i: (0, 0)), pl.BlockSpec((d, HEAD), lambda i: (0, 0))],
        out_specs=pl.BlockSpec((tm, HEAD), lambda i: (i, 0)),
        out_shape=jax.ShapeDtypeStruct((m, HEAD), F32),
        compiler_params=_params("parallel"),
        name="router",
    )(*[x for x, _ in x_parts], g.reshape(1, d), wpad)


def _prefix_incl(x01, upper):
    within = jnp.dot(x01.astype(BF), upper, preferred_element_type=F32)
    totals = jnp.broadcast_to(within[:, SEL_CHUNK - 1:SEL_CHUNK], within.shape)
    chunk = lax.broadcasted_iota(jnp.int32, within.shape, 0)
    before = jnp.zeros_like(within)
    for j in range(within.shape[0] - 1):
        before = before + jnp.where(chunk > j, totals[j:j + 1, :], 0.0)
    return within + before


def _topc_kernel(a_ref, tab_ref, idx_ref, gate_ref, *, cap, n_experts):
    a = a_ref[...]
    nch = a.shape[0]
    bits = pltpu.bitcast(a, jnp.int32)

    def count(mask):
        return jnp.sum(jnp.where(mask, 1.0, 0.0), keepdims=True)

    def keeps(cand):
        return jnp.where(count(bits >= cand) >= cap, 1, 0)

    def two_bits(i, t):
        shift = 28 - 2 * i
        digit = keeps(t | jnp.left_shift(jnp.int32(1), shift))
        for d in (2, 3):
            digit = digit + keeps(t | jnp.left_shift(jnp.int32(d), shift))
        return t | jnp.left_shift(digit, shift)

    top = jnp.left_shift(keeps(jnp.full((1, 1), 1 << 30, jnp.int32)), 30)
    thr = lax.fori_loop(0, 15, two_bits, top)
    r_i = lax.broadcasted_iota(jnp.int32, (SEL_CHUNK, SEL_CHUNK), 0)
    c_i = lax.broadcasted_iota(jnp.int32, (SEL_CHUNK, SEL_CHUNK), 1)
    upper = jnp.where(r_i <= c_i, 1.0, 0.0).astype(BF)

    gt = bits > thr
    eq = bits == thr
    room = cap - count(gt)
    eq_f = jnp.where(eq, 1.0, 0.0)
    eq_before = _prefix_incl(eq_f, upper) - eq_f
    sel = gt | (eq & (eq_before < room))
    rank = jnp.where(sel, _prefix_incl(jnp.where(sel, 1.0, 0.0), upper), 0.0)

    want = (lax.broadcasted_iota(jnp.int32, (cap, SEL_CHUNK), 0) + 1).astype(F32)
    acc = jnp.zeros((cap, HEAD), F32)
    for j in range(nch):
        hit = jnp.where(rank[j:j + 1, :] == want, 1.0, 0.0).astype(BF)
        acc = acc + jnp.dot(hit, tab_ref[j * SEL_CHUNK:(j + 1) * SEL_CHUNK, :], preferred_element_type=F32)
    lane = lax.broadcasted_iota(jnp.int32, acc.shape, 1)

    def pick(l):
        return jnp.sum(jnp.where(lane == l, acc, 0.0), axis=-1, keepdims=True)

    e = pl.program_id(0)
    gate_ref[...] = (pick(e) + pick(e + n_experts)) + pick(e + 2 * n_experts)
    idx_ref[...] = (pick(3 * n_experts + 1) * SEL_CHUNK + pick(3 * n_experts)).astype(jnp.int32)


def topc_table(aff):
    n, e = aff.shape
    assert 3 * e + 2 <= HEAD and n // SEL_CHUNK <= 256
    hi = aff.astype(BF)
    rest = aff - hi.astype(F32)
    mid = rest.astype(BF)
    lo = (rest - mid.astype(F32)).astype(BF)
    tok = jnp.arange(n, dtype=jnp.int32)
    pos = (tok % SEL_CHUNK).astype(BF)[:, None]
    chunk = (tok // SEL_CHUNK).astype(BF)[:, None]
    return jnp.concatenate([hi, mid, lo, pos, chunk, jnp.zeros((n, HEAD - 3 * e - 2), BF)], axis=1)


def topc(aff, cap):
    n, e = aff.shape
    nch = n // SEL_CHUNK
    return pl.pallas_call(
        functools.partial(_topc_kernel, cap=cap, n_experts=e),
        grid=(e,),
        in_specs=[pl.BlockSpec((None, nch, SEL_CHUNK), lambda i: (i, 0, 0)),
                  pl.BlockSpec((n, HEAD), lambda i: (0, 0))],
        out_specs=[pl.BlockSpec((None, cap, 1), lambda i: (i, 0, 0)), pl.BlockSpec((None, cap, 1), lambda i: (i, 0, 0))],
        out_shape=[jax.ShapeDtypeStruct((e, cap, 1), jnp.int32), jax.ShapeDtypeStruct((e, cap, 1), F32)],
        compiler_params=_params("parallel"),
        name="topc",
    )(aff.T.reshape(e, nch, SEL_CHUNK), topc_table(aff))


def _row_copy(src_hbm, dst, sem, tok, r):
    return pltpu.make_async_copy(src_hbm.at[pl.ds(tok, 1), :], dst.at[pl.ds(r, 1), :], sem)


ROW_LOOP_UNROLL = 8


def _gather_norm_kernel(idx_ref, *refs, tr, offs, tiles_per_list):
    srcs, (g_ref, o_ref, buf, sem) = refs[:len(offs)], refs[len(offs):]
    i = pl.program_id(0)
    slot = i % 2

    def rows(tile, s, go):
        def loop(src, off):
            def body(k, c):
                for lane in range(2):
                    r = 2 * k + lane
                    cp = _row_copy(src, buf.at[s], sem.at[s], idx_ref[tile * tr + r] - off, r)
                    cp.start(priority=lane) if go else cp.wait()
                return c
            lax.fori_loop(0, tr // 2, body, 0, unroll=ROW_LOOP_UNROLL // 2)

        if len(srcs) == 1:
            loop(srcs[0], offs[0])
            return
        group = (tile // tiles_per_list) % len(srcs)
        for p, (src, off) in enumerate(zip(srcs, offs)):
            @pl.when(group == p)
            def _():
                loop(src, off)

    @pl.when(i == 0)
    def _():
        rows(0, 0, True)

    @pl.when(i + 1 < pl.num_programs(0))
    def _():
        rows(i + 1, 1 - slot, True)

    rows(i, slot, False)
    o_ref[...] = (_rms(buf[slot]) * g_ref[...]).astype(o_ref.dtype)


def gather_norm(x_parts, g, idx, cap, cfg):
    d = x_parts[0][0].shape[1]
    r = idx.shape[0]
    tr = cfg.tr
    assert cap % tr == 0 and len(x_parts) in (1, len(cfg.groups))
    return pl.pallas_call(
        functools.partial(_gather_norm_kernel, tr=tr, offs=tuple(o for _, o in x_parts), tiles_per_list=cap // tr),
        grid_spec=pltpu.PrefetchScalarGridSpec(
            num_scalar_prefetch=1,
            grid=(r // tr,),
            in_specs=[pl.BlockSpec(memory_space=pl.ANY)] * len(x_parts) + [pl.BlockSpec((1, d), lambda i, idx: (0, 0))],
            out_specs=pl.BlockSpec((tr, d), lambda i, idx: (i, 0)),
            scratch_shapes=[pltpu.VMEM((2, tr, d), F32), pltpu.SemaphoreType.DMA((2,))],
        ),
        out_shape=jax.ShapeDtypeStruct((r, d), BF),
        compiler_params=_params("arbitrary"),
        name="gather_norm",
    )(idx, *[x for x, _ in x_parts], g.reshape(1, d))


def _ffn_up_kernel(x_ref, wg_ref, wu_ref, o_ref):
    x = x_ref[...]
    gate = jnp.dot(x, wg_ref[...].astype(BF), preferred_element_type=F32)
    up = jnp.dot(x, wu_ref[...].astype(BF), preferred_element_type=F32)
    o_ref[...] = (jax.nn.silu(gate) * up).astype(o_ref.dtype)


def ffn_up(xe, w_gate, w_up, layer, cfg):
    _, e, d, f = w_gate.shape
    rows = xe.shape[0] // e
    tf = cfg.tf
    return pl.pallas_call(
        _ffn_up_kernel,
        grid=(e, f // tf),
        in_specs=[
            pl.BlockSpec((rows, d), lambda i, j: (i, 0), pipeline_mode=pl.Buffered(1)),
            pl.BlockSpec((None, None, d, tf), lambda i, j: (layer, i, 0, j)),
            pl.BlockSpec((None, None, d, tf), lambda i, j: (layer, i, 0, j)),
        ],
        out_specs=pl.BlockSpec((rows, tf), lambda i, j: (i, j)),
        out_shape=jax.ShapeDtypeStruct((xe.shape[0], f), BF),
        compiler_params=_params("parallel", "arbitrary"),
        name="ffn_up",
    )(xe, w_gate, w_up)


def _ffn_down_kernel(h_ref, w_ref, g_ref, o_ref):
    o_ref[...] = jnp.dot(h_ref[...], w_ref[...].astype(BF), preferred_element_type=F32) * g_ref[...]


def ffn_down(hid, w_down, layer, gate, cfg):
    _, e, f, d = w_down.shape
    rows = hid.shape[0] // e
    tn = cfg.tn
    return pl.pallas_call(
        _ffn_down_kernel,
        grid=(e, d // tn),
        in_specs=[
            pl.BlockSpec((rows, f), lambda i, j: (i, 0)),
            pl.BlockSpec((None, None, f, tn), lambda i, j: (layer, i, 0, j)),
            pl.BlockSpec((rows, 1), lambda i, j: (i, 0)),
        ],
        out_specs=pl.BlockSpec((rows, tn), lambda i, j: (i, j)),
        out_shape=jax.ShapeDtypeStruct((hid.shape[0], d), F32),
        compiler_params=_params("parallel", "arbitrary"),
        name="ffn_down",
    )(hid, w_down, gate)


def _scatter_add_kernel(idx_ref, y_ref, x_hbm, o_hbm, buf, sem, *, tr, off, tile_of, tiles_per_expert):
    del x_hbm
    i = pl.program_id(0)
    tile = tile_of(i)
    slot = i % 2
    pos = i % tiles_per_expert
    first = pos == 0
    last = pos == tiles_per_expert - 1

    def fetch(t, s, r):
        tok = idx_ref[t * tr + r] - off
        return pltpu.make_async_copy(o_hbm.at[pl.ds(tok, 1), :], buf.at[s, pl.ds(r, 1), :], sem.at[s])

    def store(t, s, r):
        tok = idx_ref[t * tr + r] - off
        return pltpu.make_async_copy(buf.at[s, pl.ds(r, 1), :], o_hbm.at[pl.ds(tok, 1), :], sem.at[2 + s])

    def each(fn):
        def body(k, c):
            for lane in range(2):
                fn(2 * k + lane, lane)
            return c
        lax.fori_loop(0, tr // 2, body, 0, unroll=ROW_LOOP_UNROLL // 2)

    @pl.when(first)
    def _():
        each(lambda r, pri: fetch(tile, slot, r).start(priority=pri))

    @pl.when(jnp.logical_not(first))
    def _():
        each(lambda r, pri: store(tile - 1, 1 - slot, r).wait())

    @pl.when(jnp.logical_not(last))
    def _():
        each(lambda r, pri: fetch(tile + 1, 1 - slot, r).start(priority=pri))

    each(lambda r, pri: fetch(tile, slot, r).wait())
    buf[slot] = buf[slot] + y_ref[...]
    each(lambda r, pri: store(tile, slot, r).start(priority=pri))

    @pl.when(last)
    def _():
        each(lambda r, pri: store(tile, slot, r).wait())


def scatter_add(x_parts, ye, idx, cap, cfg):
    d = ye.shape[1]
    r = idx.shape[0]
    tr = cfg.tr
    n_parts = len(x_parts)
    per_expert = r // cfg.n_experts // tr
    mine = per_expert // n_parts
    assert cap % tr == 0 and n_parts in (1, len(cfg.groups)) and mine * n_parts == per_expert
    outs = []
    for p, (x, off) in enumerate(x_parts):
        def tile_of(i, p=p):
            return (i // mine) * per_expert + p * mine + i % mine

        outs.append(pl.pallas_call(
            functools.partial(_scatter_add_kernel, tr=tr, off=off, tile_of=tile_of, tiles_per_expert=mine),
            grid_spec=pltpu.PrefetchScalarGridSpec(
                num_scalar_prefetch=1,
                grid=(cfg.n_experts * mine,),
                in_specs=[pl.BlockSpec((tr, d), lambda i, idx, tile_of=tile_of: (tile_of(i), 0)),
                          pl.BlockSpec(memory_space=pl.ANY)],
                out_specs=pl.BlockSpec(memory_space=pl.ANY),
                scratch_shapes=[pltpu.VMEM((2, tr, d), F32), pltpu.SemaphoreType.DMA((4,))],
            ),
            out_shape=jax.ShapeDtypeStruct(x.shape, F32),
            input_output_aliases={2: 0},
            compiler_params=_params("arbitrary"),
            name="scatter_add",
        )(idx, ye, x))
    return outs


def _rope_tables(cfg):
    inv = 1.0 / (ROPE_THETA ** (jnp.arange(0, HEAD, 2, dtype=F32) / HEAD))
    cos, sin = [], []
    sign = jnp.where(jnp.arange(HEAD) < HEAD // 2, -1.0, 1.0).astype(F32)
    for batch, seq in cfg.groups:
        ang = jnp.arange(seq, dtype=F32)[:, None] * inv[None, :]
        ang = jnp.concatenate([ang, ang], axis=-1)
        cos.append(jnp.tile(jnp.cos(ang), (batch, 1)))
        sin.append(jnp.tile(jnp.sin(ang) * sign, (batch, 1)))
    return jnp.concatenate(cos, axis=0), jnp.concatenate(sin, axis=0)


def _group_offsets(cfg):
    offs, o = [], 0
    for batch, seq in cfg.groups:
        offs.append(o)
        o += batch * seq
    return offs, o


def _mixer(x_parts, l, cos, sin, p, cfg, out_parts):
    d = cfg.d_model
    offs, m = _group_offsets(cfg)
    ones = jnp.ones((MIXW,), F32)
    heads = MIXW // HEAD
    qk = p["qk_norm"][l].astype(F32)
    nw = jnp.concatenate([
        jnp.tile(qk[0] * Q_SCALE, heads), jnp.tile(qk[1], heads), ones,
        jnp.tile(qk[2] * Q_SCALE, heads), jnp.tile(qk[3], heads), ones,
        jnp.tile(qk[4] * Q_SCALE, heads), jnp.tile(qk[5], heads), ones,
        jnp.ones((3 * d,), F32)]).reshape(1, -1)
    bias = jnp.concatenate([jnp.zeros((QKV_COLS,), F32), p["b_gate"][l].astype(F32)]).reshape(1, -1)

    h = rmsnorm_bf16(x_parts, p["g_mix"][l], m, min(512, cfg.tm))
    proj = in_proj(h, p["w_in"], l, nw, bias, cos, sin, cfg)

    lam_init = 0.8 - 0.6 * math.exp(-0.3 * l)
    cbias = natten_bias(p["rpb"][l].astype(F32))
    oa, ob, oc = [], [], []
    for (batch, seq), goff in zip(cfg.groups, offs):
        oa.append((attn_a(proj, goff, batch, seq, cfg), goff))
        ob.append((attn_b(proj, p["lambda_qk"][l].astype(F32), p["subln"][l].astype(F32), lam_init, goff, batch, seq,
                          cfg), goff))
        oc.append((attn_c(proj, cbias, goff, batch, seq, cfg), goff))
    merged = merge(oa, ob, oc, p["w_br_a"], p["w_br_b"], p["w_br_c"], l, proj, cfg)
    new = out_proj(merged, p["w_out"], l, x_parts, cfg, out_parts)
    return [(x, off) for x, (_, off) in zip(new, out_parts)] if out_parts else [(new, 0)]


def _moe(x_parts, l, p, cfg):
    e = cfg.n_experts
    offs, m = _group_offsets(cfg)
    aff = router(x_parts, p["g_ffn"][l], p["w_router"][l], m, cfg)[:, :e]
    idx, gate = [], []
    caps = {CAP_FACTOR * batch * seq // e for batch, seq in cfg.groups}
    assert len(caps) == 1
    cap = caps.pop()
    for (batch, seq), goff in zip(cfg.groups, offs):
        n = batch * seq
        i_g, g_g = topc(aff[goff:goff + n], cap)
        idx.append(i_g.reshape(e, cap) + goff)
        gate.append(g_g.reshape(e, cap))
    idx = jnp.concatenate(idx, axis=1).reshape(-1)
    gate = jnp.concatenate(gate, axis=1).reshape(-1, 1)
    xe = gather_norm(x_parts, p["g_ffn"][l], idx, cap, cfg)
    hid = ffn_up(xe, p["w_e_gate"], p["w_e_up"], l, cfg)
    ye = ffn_down(hid, p["w_e_down"], l, gate, cfg)
    return [(x, off) for x, (_, off) in zip(scatter_add(x_parts, ye, idx, cap, cfg), x_parts)]


def trunk(xs, p, cfg):
    d = cfg.d_model
    offs, _ = _group_offsets(cfg)
    x_parts = [(a.reshape(-1, d), o) for a, o in zip(xs, offs)]
    by_group = [(b * s, o) for (b, s), o in zip(cfg.groups, offs)]
    cos, sin = _rope_tables(cfg)
    for l in range(cfg.depth):
        out_parts = by_group if l == cfg.depth - 1 else None
        x_parts = _moe(_mixer(x_parts, l, cos, sin, p, cfg, out_parts), l, p, cfg)
    return tuple(x.reshape(b, s, d) for (x, _), (b, s) in zip(x_parts, cfg.groups))


def kernel(x_prompt, x_sample, g_mix, w_in, b_gate, qk_norm, lambda_qk, subln, rpb, w_br_a, w_br_b, w_br_c,
           w_out, g_ffn, w_router, w_e_gate, w_e_up, w_e_down):
    cfg = Cfg(d_model=x_prompt.shape[-1],
              groups=(x_prompt.shape[:2], x_sample.shape[:2]),
              n_experts=w_router.shape[-1], d_ff=w_e_gate.shape[-1], depth=g_mix.shape[0])
    p = dict(g_mix=g_mix, w_in=w_in, b_gate=b_gate, qk_norm=qk_norm, lambda_qk=lambda_qk, subln=subln, rpb=rpb,
             w_br_a=w_br_a, w_br_b=w_br_b, w_br_c=w_br_c, w_out=w_out, g_ffn=g_ffn, w_router=w_router,
             w_e_gate=w_e_gate, w_e_up=w_e_up, w_e_down=w_e_down)
    return trunk((x_prompt, x_sample), p, cfg)
```
